```python
import jax, jax.numpy as jnp
import numpy as np

D_MODEL = 1024
BATCH = 8
SEQ = 4096
DEPTH = 4

HEAD_DIM = 64
BRANCH_WIDTH = D_MODEL
A_Q_HEADS = BRANCH_WIDTH // HEAD_DIM
A_KV_HEADS = 2
A_WINDOW = 128
B_Q_HEADS = BRANCH_WIDTH // HEAD_DIM
B_KV_HEADS = 4
B_PAIRS = ((128, 1), (512, 4), (2048, 16))
N_GROUPS = len(B_PAIRS)
BLOCK = 128
ROT_DIM = HEAD_DIM // 4
ROPE_THETA = 500000.0
EPS = 1e-6
N_MIXERS = 2
N_A = (DEPTH + 1) // 2
N_B = DEPTH // 2
A_COLS = (A_Q_HEADS + 2 * A_KV_HEADS) * HEAD_DIM + BRANCH_WIDTH
B_GROUP_COLS = (B_Q_HEADS + 2 * B_KV_HEADS) * HEAD_DIM
B_COLS = N_GROUPS * B_GROUP_COLS + BRANCH_WIDTH
SCALE = HEAD_DIM ** -0.5

kernel_name = "hybrid_swa_sink_dilated_gated_trunk"


def rmsnorm(x, g):
    xf = x.astype(jnp.float32)
    y = xf * jax.lax.rsqrt(jnp.mean(xf * xf, axis=-1, keepdims=True) + EPS)
    return (y * g.astype(jnp.float32)).astype(x.dtype)


def rope_tables(seq):
    pos = jnp.arange(seq, dtype=jnp.float32)
    inv = ROPE_THETA ** (-jnp.arange(0, ROT_DIM, 2, dtype=jnp.float32) / ROT_DIM)
    ang = pos[:, None] * inv[None, :]
    return jnp.cos(ang), jnp.sin(ang)


def partial_rope(t, cos, sin):
    tf = t.astype(jnp.float32)
    half = ROT_DIM // 2
    t1, t2, rest = tf[..., :half], tf[..., half:ROT_DIM], tf[..., ROT_DIM:]
    c = cos[None, :, None, :]
    s = sin[None, :, None, :]
    out = jnp.concatenate([t1 * c - t2 * s, t2 * c + t1 * s, rest], axis=-1)
    return out.astype(t.dtype)


def banded_attention(q, k, v, max_dist, sink=None):
    n, L, hq, d = q.shape
    hkv = k.shape[2]
    rep = hq // hkv
    nb = -(-L // BLOCK)
    lp = nb * BLOCK
    pad = ((0, 0), (0, lp - L), (0, 0), (0, 0))
    q, k, v = jnp.pad(q, pad), jnp.pad(k, pad), jnp.pad(v, pad)
    qb = q.reshape(n, nb, BLOCK, hkv, rep, d)

    def with_prev(t):
        tb = t.reshape(n, nb, BLOCK, hkv, d)
        prev = jnp.pad(tb, ((0, 0), (1, 0), (0, 0), (0, 0), (0, 0)))[:, :-1]
        return jnp.concatenate([prev, tb], axis=2)

    kk, vv = with_prev(k), with_prev(v)
    s = jnp.einsum('nbqgrd,nbkgd->nbgrqk', qb, kk,
                   preferred_element_type=jnp.float32) * SCALE
    i = jnp.arange(BLOCK)[:, None]
    j = jnp.arange(2 * BLOCK)[None, :]
    dist = BLOCK + i - j
    blk = jnp.arange(nb)[:, None, None]
    valid = (dist >= 0) & (dist <= max_dist) & ((blk > 0) | (j >= BLOCK))
    s = jnp.where(valid[None, :, None, None], s, -jnp.inf)
    m = jnp.max(s, axis=-1)
    if sink is not None:
        sk = sink.astype(jnp.float32).reshape(hkv, rep)[None, None, :, :, None]
        m = jnp.maximum(m, sk)
    p = jnp.exp(s - m[..., None])
    l = jnp.sum(p, axis=-1)
    if sink is not None:
        l = l + jnp.exp(sk - m)
    o = jnp.einsum('nbgrqk,nbkgd->nbgrqd', p, vv.astype(jnp.float32)) / l[..., None]
    o = o.transpose(0, 1, 4, 2, 3, 5).reshape(n, lp, hq, d)[:, :L]
    lse = (m + jnp.log(l)).transpose(0, 1, 4, 2, 3).reshape(n, lp, hq)[:, :L]
    return o.astype(q.dtype), lse


def dilated_group(q, k, v, window, dilation):
    b, s, hq, d = q.shape
    sub = s // dilation

    def fold(t):
        return t.reshape(b, sub, dilation, t.shape[2], d).swapaxes(1, 2).reshape(b * dilation, sub, t.shape[2], d)

    o, lse = banded_attention(fold(q), fold(k), fold(v), window // dilation)
    o = o.reshape(b, dilation, sub, hq, d).swapaxes(1, 2).reshape(b, s, hq, d)
    lse = lse.reshape(b, dilation, sub, hq).swapaxes(1, 2).reshape(b, s, hq)
    return o, lse


def mixer_a(h, w_in, q_gain, k_gain, sinks, cos, sin):
    b, s, _ = h.shape
    z = h @ w_in
    nq, nk = A_Q_HEADS * HEAD_DIM, A_KV_HEADS * HEAD_DIM
    q, k, v, gate = jnp.split(z, [nq, nq + nk, nq + 2 * nk], axis=-1)
    q = partial_rope(rmsnorm(q.reshape(b, s, A_Q_HEADS, HEAD_DIM), q_gain), cos, sin)
    k = partial_rope(rmsnorm(k.reshape(b, s, A_KV_HEADS, HEAD_DIM), k_gain), cos, sin)
    v = v.reshape(b, s, A_KV_HEADS, HEAD_DIM)
    o, _ = banded_attention(q, k, v, A_WINDOW - 1, sink=sinks)
    return o.reshape(b, s, BRANCH_WIDTH) * jax.nn.silu(gate)


def mixer_b(h, w_in, q_gain, k_gain, cos, sin):
    b, s, _ = h.shape
    z = h @ w_in
    heads = z[..., :N_GROUPS * B_GROUP_COLS].reshape(b, s, N_GROUPS, B_GROUP_COLS)
    gate = z[..., N_GROUPS * B_GROUP_COLS:]
    nq, nk = B_Q_HEADS * HEAD_DIM, B_KV_HEADS * HEAD_DIM
    outs, lses = [], []
    for g, (window, dilation) in enumerate(B_PAIRS):
        q, k, v = jnp.split(heads[:, :, g], [nq, nq + nk], axis=-1)
        q = partial_rope(rmsnorm(q.reshape(b, s, B_Q_HEADS, HEAD_DIM), q_gain[g]), cos, sin)
        k = partial_rope(rmsnorm(k.reshape(b, s, B_KV_HEADS, HEAD_DIM), k_gain[g]), cos, sin)
        v = v.reshape(b, s, B_KV_HEADS, HEAD_DIM)
        o, lse = dilated_group(q, k, v, window, dilation)
        outs.append(o)
        lses.append(lse)
    wts = jax.nn.softmax(jnp.stack(lses), axis=0)
    o = jnp.einsum('gbsh,gbshd->bshd', wts, jnp.stack(outs).astype(jnp.float32))
    return o.reshape(b, s, BRANCH_WIDTH).astype(h.dtype) * jax.nn.silu(gate)


def setup_inputs(seed: int = 0) -> dict:
    key = jax.random.key(seed)
    ks = jax.random.split(key, 12)
    f32 = jnp.float32
    x = jax.random.normal(ks[0], (BATCH, SEQ, D_MODEL), f32)
    norm_a = 1.0 + 0.02 * jax.random.normal(ks[1], (N_A, D_MODEL), f32)
    w_in_a = jax.random.normal(ks[2], (N_A, D_MODEL, A_COLS), f32) * D_MODEL ** -0.5
    q_gain_a = 1.0 + 0.02 * jax.random.normal(ks[3], (N_A, HEAD_DIM), f32)
    k_gain_a = 1.0 + 0.02 * jax.random.normal(ks[4], (N_A, HEAD_DIM), f32)
    sinks_a = jax.random.normal(ks[5], (N_A, A_Q_HEADS), f32)
    w_out_a = jax.random.normal(ks[6], (N_A, BRANCH_WIDTH, D_MODEL), f32) * BRANCH_WIDTH ** -0.5
    norm_b = 1.0 + 0.02 * jax.random.normal(ks[7], (N_B, D_MODEL), f32)
    w_in_b = jax.random.normal(ks[8], (N_B, D_MODEL, B_COLS), f32) * D_MODEL ** -0.5
    q_gain_b = 1.0 + 0.02 * jax.random.normal(ks[9], (N_B, N_GROUPS, HEAD_DIM), f32)
    k_gain_b = 1.0 + 0.02 * jax.random.normal(ks[10], (N_B, N_GROUPS, HEAD_DIM), f32)
    w_out_b = jax.random.normal(ks[11], (N_B, BRANCH_WIDTH, D_MODEL), f32) * BRANCH_WIDTH ** -0.5
    return {"x": x, "norm_a": norm_a, "w_in_a": w_in_a, "q_gain_a": q_gain_a,
            "k_gain_a": k_gain_a, "sinks_a": sinks_a, "w_out_a": w_out_a,
            "norm_b": norm_b, "w_in_b": w_in_b, "q_gain_b": q_gain_b,
            "k_gain_b": k_gain_b, "w_out_b": w_out_b}


def reference(x, norm_a, w_in_a, q_gain_a, k_gain_a, sinks_a, w_out_a,
              norm_b, w_in_b, q_gain_b, k_gain_b, w_out_b):
    cos, sin = rope_tables(x.shape[1])
    for layer in range(DEPTH):
        idx = layer // N_MIXERS
        if layer % N_MIXERS == 0:
            h = rmsnorm(x, norm_a[idx])
            y = mixer_a(h, w_in_a[idx], q_gain_a[idx], k_gain_a[idx], sinks_a[idx], cos, sin)
            x = x + y @ w_out_a[idx]
        else:
            h = rmsnorm(x, norm_b[idx])
            y = mixer_b(h, w_in_b[idx], q_gain_b[idx], k_gain_b[idx], cos, sin)
            x = x + y @ w_out_b[idx]
    return x
```

```python
import functools

import jax
import jax.numpy as jnp
from jax import lax
from jax.experimental import pallas as pl
from jax.experimental.pallas import tpu as pltpu

D_MODEL = 1024
HEAD_DIM = 64
N_HEADS = 16
BRANCH = N_HEADS * HEAD_DIM
A_KV = 2
B_KV = 4
B_PAIRS = ((128, 1), (512, 4), (2048, 16))
A_WINDOW = 128
BLOCK = 128
ROT_DIM = HEAD_DIM // 4
ROPE_THETA = 500000.0
EPS = 1e-6
SCALE = HEAD_DIM ** -0.5
LANES = 128
NEG = -1e30

PROJ_ROWS = 512
OUT_ROWS = 512
CHUNK = 256
VMEM_LIMIT = 56 * 1024 * 1024

bf16 = jnp.bfloat16
f32 = jnp.float32


def _const_spec(shape):
    nd = len(shape)
    return pl.BlockSpec(shape, lambda *_: (0,) * nd, pipeline_mode=pl.Buffered(1))


def _proj_kernel(x_ref, ng_ref, w_ref, bd_ref, c_ref, s1_ref, s2_ref, gq_ref, gk_ref,
                 *out_refs, n_groups, n_kv):
    x = x_ref[...]
    ms = jnp.mean(x * x, axis=-1, keepdims=True)
    h = ((x * lax.rsqrt(ms + EPS)) * ng_ref[...]).astype(bf16)
    bd = bd_ref[...]
    cos, sn1, sn2 = c_ref[...], s1_ref[...], s2_ref[...]

    def head_norm_rope(z, gain):
        w = z.shape[1]
        msq = jnp.dot((z * z).astype(bf16), bd[:w, :w], preferred_element_type=f32)
        t = (z * lax.rsqrt(msq + EPS)) * gain[:, :w]
        up = pltpu.roll(t, LANES - ROT_DIM // 2, axis=1) if w == LANES else jnp.concatenate(
            [pltpu.roll(t[:, i:i + LANES], LANES - ROT_DIM // 2, axis=1) for i in range(0, w, LANES)], axis=1)
        dn = pltpu.roll(t, ROT_DIM // 2, axis=1) if w == LANES else jnp.concatenate(
            [pltpu.roll(t[:, i:i + LANES], ROT_DIM // 2, axis=1) for i in range(0, w, LANES)], axis=1)
        return t * cos[:, :w] + up * sn1[:, :w] + dn * sn2[:, :w]

    def swap_halves(z):
        return jnp.concatenate(
            [pltpu.roll(z[:, i:i + LANES], LANES // 2, axis=1) for i in range(0, z.shape[1], LANES)], axis=1)

    kw = n_kv * HEAD_DIM
    col = 0
    oi = 0
    for g in range(n_groups):
        q_ref, k_ref, ksw_ref, v_ref, vsw_ref = out_refs[oi:oi + 5]
        oi += 5
        for c in range(0, BRANCH, CHUNK):
            z = jnp.dot(h, w_ref[:, col + c:col + c + CHUNK], preferred_element_type=f32)
            q_ref[:, c:c + CHUNK] = head_norm_rope(z, gq_ref[g]).astype(bf16)
        col += BRANCH
        z = jnp.dot(h, w_ref[:, col:col + kw], preferred_element_type=f32)
        k = head_norm_rope(z, gk_ref[g])
        k_ref[...] = k.astype(bf16)
        ksw_ref[...] = swap_halves(k).astype(bf16)
        col += kw
        v = jnp.dot(h, w_ref[:, col:col + kw], preferred_element_type=f32)
        v_ref[...] = v.astype(bf16)
        vsw_ref[...] = swap_halves(v).astype(bf16)
        col += kw
    sg_ref = out_refs[oi]
    for c in range(0, BRANCH, CHUNK):
        gt = jnp.dot(h, w_ref[:, col + c:col + c + CHUNK], preferred_element_type=f32)
        sg_ref[:, c:c + CHUNK] = (gt * (1.0 / (1.0 + jnp.exp(-gt)))).astype(bf16)


def _project(x2, norm_g, w, bd, rope, gq, gk, *, n_groups, n_kv, seq):
    n = x2.shape[0]
    cols = w.shape[1]
    kw = n_kv * HEAD_DIM
    t = PROJ_ROWS
    per_seq = seq // t
    row_spec = lambda wd: pl.BlockSpec((t, wd), lambda i: (i, 0))
    rope_spec = pl.BlockSpec((t, CHUNK), lambda i: (i % per_seq, 0))
    out_shapes, out_specs = [], []
    for _ in range(n_groups):
        for wd in (BRANCH, kw, kw, kw, kw):
            out_shapes.append(jax.ShapeDtypeStruct((n, wd), bf16))
            out_specs.append(row_spec(wd))
    out_shapes.append(jax.ShapeDtypeStruct((n, BRANCH), bf16))
    out_specs.append(row_spec(BRANCH))
    return pl.pallas_call(
        functools.partial(_proj_kernel, n_groups=n_groups, n_kv=n_kv),
        grid=(n // t,),
        in_specs=[row_spec(D_MODEL), _const_spec((1, D_MODEL)), _const_spec((D_MODEL, cols)),
                  _const_spec((CHUNK, CHUNK)), rope_spec, rope_spec, rope_spec,
                  _const_spec(gq.shape), _const_spec(gk.shape)],
        out_specs=out_specs,
        out_shape=out_shapes,
        compiler_params=pltpu.CompilerParams(
            dimension_semantics=("parallel",), vmem_limit_bytes=VMEM_LIMIT),
        name="proj",
    )(x2, norm_g, w, bd, *rope, gq, gk)


def _attn_kernel(q_ref, kp_ref, kc_ref, ksp_ref, ksc_ref, vp_ref, vc_ref, vsp_ref, vsc_ref,
                 bias_ref, mlo_ref, mhi_ref, *rest, n_kv, gated):
    if gated:
        sink_ref, sg_ref, y_ref = rest
    else:
        o_ref, lse_ref = rest
    rep = N_HEADS // n_kv
    npair = rep // 2
    m_rows = npair * BLOCK
    bias = bias_ref[...]
    mlo, mhi = mlo_ref[...], mhi_ref[...]
    lane_lo = lax.broadcasted_iota(jnp.int32, (m_rows, LANES), 1) < HEAD_DIM

    def both(p_ref, c_ref, sl):
        return jnp.concatenate([p_ref[:, sl], c_ref[:, sl]], axis=0)

    for kvh in range(n_kv):
        sl = slice(LANES * (kvh // 2), LANES * (kvh // 2) + LANES)
        k, ks = both(kp_ref, kc_ref, sl), both(ksp_ref, ksc_ref, sl)
        v, vs = both(vp_ref, vc_ref, sl), both(vsp_ref, vsc_ref, sl)
        if kvh % 2 == 0:
            k_lo, k_hi, v_lo, v_hi = k * mlo, ks * mhi, v * mlo, vs * mhi
        else:
            k_lo, k_hi, v_lo, v_hi = ks * mlo, k * mhi, vs * mlo, v * mhi
        vx = (jnp.concatenate([v_lo, mlo], axis=1), jnp.concatenate([v_hi, mhi], axis=1))
        kx = (k_lo, k_hi)
        p0 = kvh * npair
        qs = jnp.concatenate([q_ref[:, LANES * p:LANES * (p + 1)] for p in range(p0, p0 + npair)], axis=0)
        acc = None
        m_par, extra = [], []
        for e in range(2):
            s = lax.dot_general(qs, kx[e], (((1,), (1,)), ((), ())), preferred_element_type=f32)
            s = s.reshape(npair, BLOCK, 2 * BLOCK) + bias[None]
            m = jnp.max(s, axis=-1, keepdims=True)
            if gated:
                sk = sink_ref[2 * kvh + e][:, :, :1]
                m = jnp.maximum(m, sk)
                extra.append(jnp.exp(sk - m).reshape(m_rows, 1))
            p = jnp.exp(s - m).astype(bf16).reshape(m_rows, 2 * BLOCK)
            ol = jnp.dot(p, vx[e], preferred_element_type=f32)
            acc = ol if acc is None else acc + ol
            m_par.append(m.reshape(m_rows, 1))
        den = acc[:, LANES:]
        if gated:
            den = den + jnp.where(lane_lo, extra[0], extra[1])
        o = acc[:, :LANES] / den
        if not gated:
            lse = jnp.where(lane_lo, m_par[0], m_par[1]) + jnp.log(den)
        for i in range(npair):
            cs = slice(LANES * (p0 + i), LANES * (p0 + i + 1))
            rs = slice(BLOCK * i, BLOCK * (i + 1))
            if gated:
                y_ref[:, cs] = (o[rs] * sg_ref[:, cs].astype(f32)).astype(bf16)
            else:
                o_ref[:, cs] = o[rs].astype(bf16)
                lse_ref[:, cs] = lse[rs]


def _attention(q, k, ksw, v, vsw, bias, mlo, mhi, *, batch, seq, dilation, n_kv, sinks=None, sg=None):
    d = dilation
    sub = seq // d
    nb = sub // BLOCK
    kw = n_kv * HEAD_DIM
    gated = sinks is not None
    view = lambda a: a.reshape(batch, sub, d * a.shape[1])
    cur = lambda wd: pl.BlockSpec((None, BLOCK, wd), lambda b, c, j: (b, j, c))
    prev = lambda wd: pl.BlockSpec((None, BLOCK, wd), lambda b, c, j: (b, jnp.maximum(j - 1, 0), c))
    bias_spec = pl.BlockSpec((None, BLOCK, 2 * BLOCK), lambda b, c, j: (jnp.minimum(j, 1), 0, 0))
    in_specs = [cur(BRANCH), prev(kw), cur(kw), prev(kw), cur(kw), prev(kw), cur(kw), prev(kw), cur(kw),
                bias_spec, _const_spec(mlo.shape), _const_spec(mhi.shape)]
    args = [view(q), view(k), view(k), view(ksw), view(ksw), view(v), view(v), view(vsw), view(vsw),
            bias, mlo, mhi]
    if gated:
        in_specs += [_const_spec(sinks.shape), cur(BRANCH)]
        args += [sinks, view(sg)]
        out_shape = jax.ShapeDtypeStruct((batch, sub, d * BRANCH), bf16)
        out_specs = cur(BRANCH)
    else:
        out_shape = [jax.ShapeDtypeStruct((batch, sub, d * BRANCH), bf16),
                     jax.ShapeDtypeStruct((batch, sub, d * BRANCH), f32)]
        out_specs = [cur(BRANCH), cur(BRANCH)]
    out = pl.pallas_call(
        functools.partial(_attn_kernel, n_kv=n_kv, gated=gated),
        grid=(batch, d, nb),
        in_specs=in_specs,
        out_specs=out_specs,
        out_shape=out_shape,
        compiler_params=pltpu.CompilerParams(
            dimension_semantics=("parallel", "parallel", "arbitrary"), vmem_limit_bytes=VMEM_LIMIT),
        name=f"attn_d{d}",
    )(*args)
    unview = lambda a: a.reshape(batch * seq, BRANCH)
    return unview(out) if gated else tuple(unview(a) for a in out)


def _out_kernel_a(x_ref, y_ref, w_ref, o_ref):
    y = y_ref[...]
    for c in range(0, D_MODEL, CHUNK):
        o_ref[:, c:c + CHUNK] = x_ref[:, c:c + CHUNK] + jnp.dot(
            y, w_ref[:, c:c + CHUNK], preferred_element_type=f32)


def _out_kernel_b(x_ref, o0_ref, o1_ref, o2_ref, l0_ref, l1_ref, l2_ref, sg_ref, w_ref, o_ref):
    l0, l1, l2 = l0_ref[...], l1_ref[...], l2_ref[...]
    mx = jnp.maximum(jnp.maximum(l0, l1), l2)
    e0, e1, e2 = jnp.exp(l0 - mx), jnp.exp(l1 - mx), jnp.exp(l2 - mx)
    num = e0 * o0_ref[...].astype(f32) + e1 * o1_ref[...].astype(f32) + e2 * o2_ref[...].astype(f32)
    y = ((num / (e0 + e1 + e2)) * sg_ref[...].astype(f32)).astype(bf16)
    for c in range(0, D_MODEL, CHUNK):
        o_ref[:, c:c + CHUNK] = x_ref[:, c:c + CHUNK] + jnp.dot(
            y, w_ref[:, c:c + CHUNK], preferred_element_type=f32)


def _out_project(x2, w, *branch_inputs):
    n = x2.shape[0]
    t = OUT_ROWS
    row = pl.BlockSpec((t, D_MODEL), lambda i: (i, 0))
    kern = _out_kernel_a if len(branch_inputs) == 1 else _out_kernel_b
    return pl.pallas_call(
        kern,
        grid=(n // t,),
        in_specs=[row] * (1 + len(branch_inputs)) + [_const_spec(w.shape)],
        out_specs=row,
        out_shape=jax.ShapeDtypeStruct((n, D_MODEL), f32),
        input_output_aliases={0: 0},
        compiler_params=pltpu.CompilerParams(
            dimension_semantics=("parallel",), vmem_limit_bytes=VMEM_LIMIT),
        name="outproj",
    )(x2, *branch_inputs, w)


def _rope_tables(seq):
    pos = jnp.arange(seq, dtype=f32)
    inv = ROPE_THETA ** (-jnp.arange(0, ROT_DIM, 2, dtype=f32) / ROT_DIM)
    ang = pos[:, None] * inv[None, :]
    cos, sin = jnp.cos(ang), jnp.sin(ang)
    half = ROT_DIM // 2
    pad = HEAD_DIM - ROT_DIM
    one = jnp.ones((seq, pad), f32)
    zero = jnp.zeros((seq, pad), f32)
    zh = jnp.zeros((seq, half), f32)
    c = jnp.concatenate([cos, cos, one], axis=1)
    s1 = jnp.concatenate([-sin, zh, zero], axis=1)
    s2 = jnp.concatenate([zh, sin, zero], axis=1)
    tile = lambda a: jnp.tile(a, (1, CHUNK // HEAD_DIM))
    return tile(c), tile(s1), tile(s2)


def _bias_tables(max_dist):
    i = jnp.arange(BLOCK)[:, None]
    j = jnp.arange(2 * BLOCK)[None, :]
    dist = BLOCK + i - j
    ok = (dist >= 0) & (dist <= max_dist)
    first = ok & (j >= BLOCK)
    return jnp.stack([jnp.where(first, 0.0, NEG), jnp.where(ok, 0.0, NEG)]).astype(f32)


def kernel(x, norm_a, w_in_a, q_gain_a, k_gain_a, sinks_a, w_out_a,
           norm_b, w_in_b, q_gain_b, k_gain_b, w_out_b):
    batch, seq, _ = x.shape
    n = batch * seq
    rope = _rope_tables(seq)
    hid = jnp.arange(CHUNK) // HEAD_DIM
    bd = jnp.where(hid[:, None] == hid[None, :], 1.0 / HEAD_DIM, 0.0).astype(bf16)
    lane = jnp.arange(LANES)[None, :]
    mlo = jnp.broadcast_to(lane < HEAD_DIM, (2 * BLOCK, LANES)).astype(bf16)
    mhi = jnp.broadcast_to(lane >= HEAD_DIM, (2 * BLOCK, LANES)).astype(bf16)
    bias_a = _bias_tables(A_WINDOW - 1)
    bias_b = _bias_tables(BLOCK)
    tile_gain = lambda g: jnp.tile(g, (1, CHUNK // HEAD_DIM))[:, None, :]

    x2 = x.reshape(n, D_MODEL)
    depth = norm_a.shape[0] + norm_b.shape[0]
    for layer in range(depth):
        idx = layer // 2
        if layer % 2 == 0:
            q, k, ksw, v, vsw, sg = _project(
                x2, norm_a[idx][None], w_in_a[idx].astype(bf16), bd, rope,
                tile_gain(q_gain_a[idx][None] * SCALE), tile_gain(k_gain_a[idx][None]),
                n_groups=1, n_kv=A_KV, seq=seq)
            npair = N_HEADS // A_KV // 2
            sk = sinks_a[idx].reshape(A_KV, npair, 2).transpose(0, 2, 1).reshape(2 * A_KV, npair)
            sk = jnp.broadcast_to(sk[:, :, None, None], (2 * A_KV, npair, 1, LANES)).astype(f32)
            y = _attention(q, k, ksw, v, vsw, bias_a, mlo, mhi, batch=batch, seq=seq,
                           dilation=1, n_kv=A_KV, sinks=sk, sg=sg)
            x2 = _out_project(x2, w_out_a[idx].astype(bf16), y)
        else:
            outs = _project(
                x2, norm_b[idx][None], w_in_b[idx].astype(bf16), bd, rope,
                tile_gain(q_gain_b[idx] * SCALE), tile_gain(k_gain_b[idx]),
                n_groups=len(B_PAIRS), n_kv=B_KV, seq=seq)
            sg = outs[-1]
            os_, ls_ = [], []
            for g, (window, dilation) in enumerate(B_PAIRS):
                q, k, ksw, v, vsw = outs[5 * g:5 * g + 5]
                assert window // dilation == BLOCK
                o, lse = _attention(q, k, ksw, v, vsw, bias_b, mlo, mhi, batch=batch, seq=seq,
                                    dilation=dilation, n_kv=B_KV)
                os_.append(o)
                ls_.append(lse)
            x2 = _out_project(x2, w_out_b[idx].astype(bf16), *os_, *ls_, sg)
    return x2.reshape(batch, seq, D_MODEL)
```

```python
import functools
import math

import jax
import jax.numpy as jnp
from jax import lax
from jax.experimental import pallas as pl
from jax.experimental.pallas import tpu as pltpu

D_MODEL = 1024
HEAD_DIM = 64
N_HEADS = 16
BRANCH = N_HEADS * HEAD_DIM
A_KV = 2
B_KV = 4
B_PAIRS = ((128, 1), (512, 4), (2048, 16))
A_WINDOW = 128
BLOCK = 128
ROT_DIM = HEAD_DIM // 4
ROPE_THETA = 500000.0
EPS = 1e-6
SCALE = HEAD_DIM ** -0.5
LOG2E = math.log2(math.e)
LANES = 128
Q_SLABS = BRANCH // LANES
NEG = -1e30

PROJ_ROWS = 512
OUT_ROWS = 512
ATT_ROWS = 512
CHUNK = 256
VMEM_LIMIT = 56 * 1024 * 1024

bf16 = jnp.bfloat16
f32 = jnp.float32


def _const_spec(shape):
    nd = len(shape)
    return pl.BlockSpec(shape, lambda *_: (0,) * nd, pipeline_mode=pl.Buffered(1))


def _lane_tiles(z):
    return [z[:, i:i + LANES] for i in range(0, z.shape[1], LANES)]


def _proj_kernel(x_ref, ng_ref, w_ref, bd_ref, c_ref, s1_ref, s2_ref, gq_ref, gk_ref,
                 *out_refs, slabbed, n_kv):
    x = x_ref[...]
    ms = jnp.mean(x * x, axis=-1, keepdims=True)
    h = ((x * lax.rsqrt(ms + EPS)) * ng_ref[...]).astype(bf16)
    bd = bd_ref[...]
    cos, sn1, sn2 = c_ref[...], s1_ref[...], s2_ref[...]

    def roll_tiles(t, shift):
        return jnp.concatenate([pltpu.roll(a, shift, axis=1) for a in _lane_tiles(t)], axis=1)

    def head_norm_rope(z, gain):
        w = z.shape[1]
        msq = jnp.dot((z * z).astype(bf16), bd[:w, :w], preferred_element_type=f32)
        t = (z * lax.rsqrt(msq + EPS)) * gain[:, :w]
        up = roll_tiles(t, LANES - ROT_DIM // 2)
        dn = roll_tiles(t, ROT_DIM // 2)
        return t * cos[:, :w] + up * sn1[:, :w] + dn * sn2[:, :w]

    def put(ref, c, val, slab):
        if slab:
            for j, a in enumerate(_lane_tiles(val)):
                ref[c // LANES + j] = a
        else:
            ref[:, c:c + val.shape[1]] = val.astype(bf16)

    kw = n_kv * HEAD_DIM
    col = 0
    oi = 0
    for g, slab in enumerate(slabbed):
        q_ref, k_ref, ksw_ref, v_ref, vsw_ref = out_refs[oi:oi + 5]
        oi += 5
        for c in range(0, BRANCH, CHUNK):
            z = jnp.dot(h, w_ref[:, col + c:col + c + CHUNK], preferred_element_type=f32)
            put(q_ref, c, head_norm_rope(z, gq_ref[g]), slab)
        col += BRANCH
        z = jnp.dot(h, w_ref[:, col:col + kw], preferred_element_type=f32)
        k = head_norm_rope(z, gk_ref[g])
        put(k_ref, 0, k, slab)
        put(ksw_ref, 0, roll_tiles(k, LANES // 2), slab)
        col += kw
        v = jnp.dot(h, w_ref[:, col:col + kw], preferred_element_type=f32)
        put(v_ref, 0, v, slab)
        put(vsw_ref, 0, roll_tiles(v, LANES // 2), slab)
        col += kw
    sg_ref = out_refs[oi]
    for c in range(0, BRANCH, CHUNK):
        gt = jnp.dot(h, w_ref[:, col + c:col + c + CHUNK], preferred_element_type=f32)
        sg_ref[:, c:c + CHUNK] = (gt * (1.0 / (1.0 + jnp.exp(-gt)))).astype(bf16)


def _project(x2, norm_g, w, bd, rope, gq, gk, *, slabbed, n_kv, seq):
    n = x2.shape[0]
    cols = w.shape[1]
    kw = n_kv * HEAD_DIM
    t = PROJ_ROWS
    per_seq = seq // t
    row_spec = lambda wd: pl.BlockSpec((t, wd), lambda i: (i, 0))
    slab_spec = lambda wd: pl.BlockSpec((wd // LANES, t, LANES), lambda i: (0, i, 0))
    rope_spec = pl.BlockSpec((t, CHUNK), lambda i: (i % per_seq, 0))
    out_shapes, out_specs = [], []
    for slab in slabbed:
        for wd in (BRANCH, kw, kw, kw, kw):
            if slab:
                out_shapes.append(jax.ShapeDtypeStruct((wd // LANES, n, LANES), f32))
                out_specs.append(slab_spec(wd))
            else:
                out_shapes.append(jax.ShapeDtypeStruct((n, wd), bf16))
                out_specs.append(row_spec(wd))
    out_shapes.append(jax.ShapeDtypeStruct((n, BRANCH), bf16))
    out_specs.append(row_spec(BRANCH))
    return pl.pallas_call(
        functools.partial(_proj_kernel, slabbed=slabbed, n_kv=n_kv),
        grid=(n // t,),
        in_specs=[row_spec(D_MODEL), _const_spec((1, D_MODEL)), _const_spec((D_MODEL, cols)),
                  _const_spec((CHUNK, CHUNK)), rope_spec, rope_spec, rope_spec,
                  _const_spec(gq.shape), _const_spec(gk.shape)],
        out_specs=out_specs,
        out_shape=out_shapes,
        compiler_params=pltpu.CompilerParams(
            dimension_semantics=("parallel",), vmem_limit_bytes=VMEM_LIMIT),
        name="proj",
    )(x2, norm_g, w, bd, *rope, gq, gk)


def _slab_attention(q_tiles, k2, ks2, v2, vs2, eye, bias_t, mlo, mhi, sinks):
    npair = len(q_tiles) // 2
    lane_lo = lax.broadcasted_iota(jnp.int32, (BLOCK, LANES), 1) < HEAD_DIM
    variants = ((k2 * mlo, ks2 * mhi, v2 * mlo, vs2 * mhi),
                (ks2 * mlo, k2 * mhi, vs2 * mlo, v2 * mhi))
    results = []
    for kv in range(2):
        k_lo, k_hi, v_lo, v_hi = variants[kv]
        kx = (jnp.concatenate([k_lo, bias_t], axis=1), jnp.concatenate([k_hi, bias_t], axis=1))
        vx = (jnp.concatenate([v_lo, mlo], axis=1), jnp.concatenate([v_hi, mhi], axis=1))
        tiles = q_tiles[kv * npair:(kv + 1) * npair]
        qs = jnp.concatenate([jnp.concatenate([t, eye], axis=1) for t in tiles], axis=0)
        acc = None
        m_par, x_par = [], []
        for e in range(2):
            s_all = lax.dot_general(qs, kx[e], (((1,), (1,)), ((), ())), preferred_element_type=f32)
            ps, ms, xs = [], [], []
            for i in range(npair):
                s = s_all[BLOCK * i:BLOCK * (i + 1)]
                m = jnp.max(s, axis=-1, keepdims=True)
                if sinks is not None:
                    sk = sinks[kv * npair + i][e]
                    m = jnp.maximum(m, sk)
                    xs.append(jnp.exp2(sk - m))
                ps.append(jnp.exp2(s - m).astype(bf16))
                ms.append(m)
            ol = jnp.dot(jnp.concatenate(ps, axis=0), vx[e], preferred_element_type=f32)
            acc = ol if acc is None else acc + ol
            m_par.append(ms)
            x_par.append(xs)
        for i in range(npair):
            a = acc[BLOCK * i:BLOCK * (i + 1)]
            den = a[:, LANES:]
            if sinks is not None:
                den = den + jnp.where(lane_lo, x_par[0][i], x_par[1][i])
            o = a[:, :LANES] / den
            lse2 = jnp.where(lane_lo, m_par[0][i], m_par[1][i]) + jnp.log2(den)
            results.append((o, lse2))
    return results


def _compact_lse(lses):
    lane = lax.broadcasted_iota(jnp.int32, (BLOCK, LANES), 1) % HEAD_DIM
    out = jnp.zeros((BLOCK, LANES), f32)
    for j, l in enumerate(lses):
        out = jnp.where(lane == j, l, out)
    return out


def _attn_dense_kernel(*refs, n_kv, gated):
    (q_ref, kp_ref, kc_ref, ksp_ref, ksc_ref, vp_ref, vc_ref, vsp_ref, vsc_ref,
     eye_ref, bias_ref, mlo_ref, mhi_ref) = refs[:13]
    if gated:
        sink_ref, sg_ref, y_ref = refs[13:16]
        scratch = refs[16:]
    else:
        o_ref, lse_ref = refs[13:15]
        scratch = refs[15:]
    kf, ksf, vf, vsf = scratch
    n_slab = n_kv // 2
    tiles_per_slab = Q_SLABS // n_slab
    for full, p_ref, c_ref in ((kf, kp_ref, kc_ref), (ksf, ksp_ref, ksc_ref),
                               (vf, vp_ref, vc_ref), (vsf, vsp_ref, vsc_ref)):
        full[0:BLOCK] = p_ref[...]
        full[BLOCK:BLOCK + ATT_ROWS] = c_ref[...]
    eye, mlo, mhi = eye_ref[...], mlo_ref[...], mhi_ref[...]
    first_step = pl.program_id(1) == 0

    def block(i, carry):
        r0 = pl.multiple_of(i * BLOCK, BLOCK)
        rq = pl.ds(r0, BLOCK)
        rk = pl.ds(r0, 2 * BLOCK)
        bias_t = bias_ref[jnp.where(jnp.logical_and(first_step, i == 0), 0, 1)]
        for sl in range(n_slab):
            ls = slice(LANES * sl, LANES * (sl + 1))
            t0 = sl * tiles_per_slab
            q_tiles = [q_ref[rq, LANES * j:LANES * (j + 1)] for j in range(t0, t0 + tiles_per_slab)]
            sinks = None
            if gated:
                sinks = [(sink_ref[2 * j], sink_ref[2 * j + 1]) for j in range(t0, t0 + tiles_per_slab)]
            res = _slab_attention(q_tiles, kf[rk, ls], ksf[rk, ls], vf[rk, ls], vsf[rk, ls],
                                  eye, bias_t, mlo, mhi, sinks)
            for j, (o, lse2) in enumerate(res):
                cs = slice(LANES * (t0 + j), LANES * (t0 + j + 1))
                if gated:
                    y_ref[rq, cs] = (o * sg_ref[rq, cs].astype(f32)).astype(bf16)
                else:
                    o_ref[rq, cs] = o.astype(bf16)
            if not gated:
                lse_ref[sl, rq, :] = _compact_lse([l for _, l in res])
        return carry

    lax.fori_loop(0, ATT_ROWS // BLOCK, block, 0)


def _attention_dense(q, k, ksw, v, vsw, consts, *, batch, seq, n_kv, sinks=None, sg=None):
    n = batch * seq
    kw = n_kv * HEAD_DIM
    gated = sinks is not None
    per_seq = seq // ATT_ROWS
    ratio = ATT_ROWS // BLOCK
    cur = lambda wd: pl.BlockSpec((ATT_ROWS, wd), lambda b, j: (b * per_seq + j, 0))
    prev = lambda wd: pl.BlockSpec(
        (BLOCK, wd), lambda b, j: ((b * per_seq + j) * ratio - jnp.minimum(j, 1), 0))
    eye, bias, mlo, mhi = consts
    in_specs = [cur(BRANCH), prev(kw), cur(kw), prev(kw), cur(kw), prev(kw), cur(kw), prev(kw), cur(kw),
                _const_spec(eye.shape), _const_spec(bias.shape), _const_spec(mlo.shape),
                _const_spec(mhi.shape)]
    args = [q, k, k, ksw, ksw, v, v, vsw, vsw, eye, bias, mlo, mhi]
    if gated:
        in_specs += [pl.BlockSpec(memory_space=pltpu.SMEM), cur(BRANCH)]
        args += [sinks, sg]
        out_shape = jax.ShapeDtypeStruct((n, BRANCH), bf16)
        out_specs = cur(BRANCH)
    else:
        out_shape = [jax.ShapeDtypeStruct((n, BRANCH), bf16),
                     jax.ShapeDtypeStruct((n_kv // 2, n, LANES), f32)]
        out_specs = [cur(BRANCH),
                     pl.BlockSpec((n_kv // 2, ATT_ROWS, LANES), lambda b, j: (0, b * per_seq + j, 0))]
    return pl.pallas_call(
        functools.partial(_attn_dense_kernel, n_kv=n_kv, gated=gated),
        grid=(batch, per_seq),
        in_specs=in_specs,
        out_specs=out_specs,
        out_shape=out_shape,
        scratch_shapes=[pltpu.VMEM((BLOCK + ATT_ROWS, kw), bf16)] * 4,
        compiler_params=pltpu.CompilerParams(
            dimension_semantics=("parallel", "arbitrary"), vmem_limit_bytes=VMEM_LIMIT),
        name="attn_dense_a" if gated else "attn_dense_b",
    )(*args)


def _attn_strided_kernel(q_ref, kp_ref, kc_ref, ksp_ref, ksc_ref, vp_ref, vc_ref, vsp_ref, vsc_ref,
                         eye_ref, bias_ref, mlo_ref, mhi_ref, o_ref, lse_ref, *, dilation):
    eye, mlo, mhi = eye_ref[...], mlo_ref[...], mhi_ref[...]
    bias_t = bias_ref[jnp.minimum(pl.program_id(1), 1)]
    n_tiles = q_ref.shape[0]

    def residue(c, carry):
        rows = pl.ds(c, BLOCK, stride=dilation)
        both = lambda p_ref, c_ref: jnp.concatenate([p_ref[0, rows, :], c_ref[0, rows, :]], axis=0).astype(bf16)
        q_tiles = [q_ref[j, rows, :].astype(bf16) for j in range(n_tiles)]
        res = _slab_attention(q_tiles, both(kp_ref, kc_ref), both(ksp_ref, ksc_ref),
                              both(vp_ref, vc_ref), both(vsp_ref, vsc_ref), eye, bias_t, mlo, mhi, None)
        for j, (o, _) in enumerate(res):
            o_ref[j, rows, :] = o
        lse_ref[0, rows, :] = _compact_lse([l for _, l in res])
        return carry

    lax.fori_loop(0, dilation, residue, 0)


def _attention_strided(q, k, ksw, v, vsw, consts, *, batch, seq, dilation, n_kv):
    n = batch * seq
    n_slab = n_kv // 2
    rows = BLOCK * dilation
    per_seq = seq // rows
    tiles = Q_SLABS // n_slab
    cur = lambda nt: pl.BlockSpec((nt, rows, LANES), lambda b, j, s: (s, b * per_seq + j, 0))
    prev = lambda nt: pl.BlockSpec((nt, rows, LANES), lambda b, j, s: (s, b * per_seq + jnp.maximum(j - 1, 0), 0))
    eye, bias, mlo, mhi = consts
    return pl.pallas_call(
        functools.partial(_attn_strided_kernel, dilation=dilation),
        grid=(batch, per_seq, n_slab),
        in_specs=[cur(tiles), prev(1), cur(1), prev(1), cur(1), prev(1), cur(1), prev(1), cur(1),
                  _const_spec(eye.shape), _const_spec(bias.shape), _const_spec(mlo.shape),
                  _const_spec(mhi.shape)],
        out_specs=[cur(tiles), cur(1)],
        out_shape=[jax.ShapeDtypeStruct((Q_SLABS, n, LANES), f32),
                   jax.ShapeDtypeStruct((n_slab, n, LANES), f32)],
        compiler_params=pltpu.CompilerParams(
            dimension_semantics=("parallel", "arbitrary", "arbitrary"), vmem_limit_bytes=VMEM_LIMIT),
        name=f"attn_strided_d{dilation}",
    )(q, k, k, ksw, ksw, v, v, vsw, vsw, eye, bias, mlo, mhi)


def _out_kernel_a(x_ref, y_ref, w_ref, o_ref):
    y = y_ref[...]
    for c in range(0, D_MODEL, CHUNK):
        o_ref[:, c:c + CHUNK] = x_ref[:, c:c + CHUNK] + jnp.dot(
            y, w_ref[:, c:c + CHUNK], preferred_element_type=f32)


def _out_kernel_b(x_ref, o0_ref, l0_ref, o1_ref, l1_ref, o2_ref, l2_ref, sg_ref, ex_ref, w_ref, o_ref):
    n_slab = l0_ref.shape[0]
    half = BRANCH // n_slab
    ex = ex_ref[...]
    wts = [[], [], []]
    for s in range(n_slab):
        l0, l1, l2 = l0_ref[s], l1_ref[s], l2_ref[s]
        mx = jnp.maximum(jnp.maximum(l0, l1), l2)
        e = [jnp.exp2(l0 - mx), jnp.exp2(l1 - mx), jnp.exp2(l2 - mx)]
        inv = 1.0 / (e[0] + e[1] + e[2])
        for g in range(3):
            wg = e[g] * inv
            hi = wg.astype(bf16)
            lo = (wg - hi.astype(f32)).astype(bf16)
            wts[g].append(jnp.dot(jnp.concatenate([hi, lo], axis=1), ex, preferred_element_type=f32))
    w0, w1, w2 = (jnp.concatenate(w, axis=1) for w in wts)
    o1 = jnp.concatenate([o1_ref[j] for j in range(Q_SLABS)], axis=1)
    o2 = jnp.concatenate([o2_ref[j] for j in range(Q_SLABS)], axis=1)
    y = ((w0 * o0_ref[...].astype(f32) + w1 * o1 + w2 * o2) * sg_ref[...].astype(f32)).astype(bf16)
    for c in range(0, D_MODEL, CHUNK):
        o_ref[:, c:c + CHUNK] = x_ref[:, c:c + CHUNK] + jnp.dot(
            y, w_ref[:, c:c + CHUNK], preferred_element_type=f32)


def _out_project(x2, w, *branch_inputs):
    n = x2.shape[0]
    t = OUT_ROWS
    row = pl.BlockSpec((t, D_MODEL), lambda i: (i, 0))
    slab = lambda a: pl.BlockSpec((a.shape[0], t, LANES), lambda i: (0, i, 0))
    if len(branch_inputs) == 1:
        kern, specs = _out_kernel_a, [row]
    else:
        kern = _out_kernel_b
        specs = [row if a.ndim == 2 and a.shape[0] == n else
                 (slab(a) if a.ndim == 3 else _const_spec(a.shape)) for a in branch_inputs]
    return pl.pallas_call(
        kern,
        grid=(n // t,),
        in_specs=[row] + specs + [_const_spec(w.shape)],
        out_specs=row,
        out_shape=jax.ShapeDtypeStruct((n, D_MODEL), f32),
        input_output_aliases={0: 0},
        compiler_params=pltpu.CompilerParams(
            dimension_semantics=("parallel",), vmem_limit_bytes=VMEM_LIMIT),
        name="outproj",
    )(x2, *branch_inputs, w)


def _rope_tables(seq):
    pos = jnp.arange(seq, dtype=f32)
    inv = ROPE_THETA ** (-jnp.arange(0, ROT_DIM, 2, dtype=f32) / ROT_DIM)
    ang = pos[:, None] * inv[None, :]
    cos, sin = jnp.cos(ang), jnp.sin(ang)
    half = ROT_DIM // 2
    pad = HEAD_DIM - ROT_DIM
    one = jnp.ones((seq, pad), f32)
    zero = jnp.zeros((seq, pad), f32)
    zh = jnp.zeros((seq, half), f32)
    c = jnp.concatenate([cos, cos, one], axis=1)
    s1 = jnp.concatenate([-sin, zh, zero], axis=1)
    s2 = jnp.concatenate([zh, sin, zero], axis=1)
    tile = lambda a: jnp.tile(a, (1, CHUNK // HEAD_DIM))
    return tile(c), tile(s1), tile(s2)


def _bias_tables(max_dist):
    i = jnp.arange(BLOCK)[None, :]
    j = jnp.arange(2 * BLOCK)[:, None]
    dist = BLOCK + i - j
    ok = (dist >= 0) & (dist <= max_dist)
    first = ok & (j >= BLOCK)
    return jnp.stack([jnp.where(first, 0.0, NEG), jnp.where(ok, 0.0, NEG)]).astype(bf16)


def kernel(x, norm_a, w_in_a, q_gain_a, k_gain_a, sinks_a, w_out_a,
           norm_b, w_in_b, q_gain_b, k_gain_b, w_out_b):
    batch, seq, _ = x.shape
    n = batch * seq
    rope = _rope_tables(seq)
    hid = jnp.arange(CHUNK) // HEAD_DIM
    bd = jnp.where(hid[:, None] == hid[None, :], 1.0 / HEAD_DIM, 0.0).astype(bf16)
    lane = jnp.arange(LANES)[None, :]
    mlo = jnp.broadcast_to(lane < HEAD_DIM, (2 * BLOCK, LANES)).astype(bf16)
    mhi = jnp.broadcast_to(lane >= HEAD_DIM, (2 * BLOCK, LANES)).astype(bf16)
    eye = jnp.eye(BLOCK, dtype=bf16)
    consts_a = (eye, _bias_tables(A_WINDOW - 1), mlo, mhi)
    consts_b = (eye, _bias_tables(BLOCK), mlo, mhi)
    half = BRANCH // (B_KV // 2)
    col_tile = jnp.arange(half)[None, :] // LANES
    col_odd = (jnp.arange(half)[None, :] % LANES) // HEAD_DIM
    row = jnp.arange(LANES)[:, None]
    ex = ((row % HEAD_DIM == col_tile) & (row // HEAD_DIM == col_odd)).astype(bf16)
    ex = jnp.concatenate([ex, ex], axis=0)
    tile_gain = lambda g: jnp.tile(g, (1, CHUNK // HEAD_DIM))[:, None, :]
    q_scale = SCALE * LOG2E

    x2 = x.reshape(n, D_MODEL)
    depth = norm_a.shape[0] + norm_b.shape[0]
    for layer in range(depth):
        idx = layer // 2
        if layer % 2 == 0:
            q, k, ksw, v, vsw, sg = _project(
                x2, norm_a[idx][None], w_in_a[idx].astype(bf16), bd, rope,
                tile_gain(q_gain_a[idx][None] * q_scale), tile_gain(k_gain_a[idx][None]),
                slabbed=(False,), n_kv=A_KV, seq=seq)
            y = _attention_dense(q, k, ksw, v, vsw, consts_a, batch=batch, seq=seq, n_kv=A_KV,
                                 sinks=sinks_a[idx].astype(f32) * LOG2E, sg=sg)
            x2 = _out_project(x2, w_out_a[idx].astype(bf16), y)
        else:
            slabbed = tuple(d > 1 for _, d in B_PAIRS)
            outs = _project(
                x2, norm_b[idx][None], w_in_b[idx].astype(bf16), bd, rope,
                tile_gain(q_gain_b[idx] * q_scale), tile_gain(k_gain_b[idx]),
                slabbed=slabbed, n_kv=B_KV, seq=seq)
            sg = outs[-1]
            merged = []
            for g, (window, dilation) in enumerate(B_PAIRS):
                assert window // dilation == BLOCK
                qkv = outs[5 * g:5 * g + 5]
                if dilation == 1:
                    merged += _attention_dense(*qkv, consts_b, batch=batch, seq=seq, n_kv=B_KV)
                else:
                    merged += _attention_strided(*qkv, consts_b, batch=batch, seq=seq,
                                                 dilation=dilation, n_kv=B_KV)
            x2 = _out_project(x2, w_out_b[idx].astype(bf16), *merged, sg, ex)
    return x2.reshape(batch, seq, D_MODEL)
```

```python
import functools
import math

import jax
import jax.numpy as jnp
from jax import lax
from jax.experimental import pallas as pl
from jax.experimental.pallas import tpu as pltpu

D_MODEL = 1024
HEAD_DIM = 64
N_HEADS = 16
BRANCH = N_HEADS * HEAD_DIM
A_KV = 2
B_KV = 4
B_PAIRS = ((128, 1), (512, 4), (2048, 16))
A_WINDOW = 128
BLOCK = 128
ROT_DIM = HEAD_DIM // 4
ROPE_THETA = 500000.0
EPS = 1e-6
SCALE = HEAD_DIM ** -0.5
LOG2E = math.log2(math.e)
LANES = 128
Q_SLABS = BRANCH // LANES
NEG = -1e30

PROJ_ROWS = 512
OUT_ROWS = 512
ATT_ROWS = 512
CHUNK = 256
WIDE = 512
VMEM_LIMIT = 56 * 1024 * 1024

bf16 = jnp.bfloat16
f32 = jnp.float32


def _const_spec(shape):
    nd = len(shape)
    return pl.BlockSpec(shape, lambda *_: (0,) * nd, pipeline_mode=pl.Buffered(1))


def _lane_tiles(z):
    return [z[:, i:i + LANES] for i in range(0, z.shape[1], LANES)]


def _proj_kernel(x_ref, ng_ref, w_ref, bd_ref, c_ref, s1_ref, s2_ref, gq_ref, gk_ref,
                 *out_refs, slabbed, n_kv):
    x = x_ref[...]
    ms = jnp.mean(x * x, axis=-1, keepdims=True)
    h = ((x * lax.rsqrt(ms + EPS)) * ng_ref[...]).astype(bf16)
    bd = bd_ref[...]
    cos, sn1, sn2 = c_ref[...], s1_ref[...], s2_ref[...]

    def roll_tiles(t, shift):
        return jnp.concatenate([pltpu.roll(a, shift, axis=1) for a in _lane_tiles(t)], axis=1)

    def head_norm_rope(z, gain):
        w = z.shape[1]
        msq = jnp.dot((z * z).astype(bf16), bd[:w, :w], preferred_element_type=f32)
        t = (z * lax.rsqrt(msq + EPS)) * gain[:, :w]
        up = roll_tiles(t, LANES - ROT_DIM // 2)
        dn = roll_tiles(t, ROT_DIM // 2)
        return t * cos[:, :w] + up * sn1[:, :w] + dn * sn2[:, :w]

    def put(ref, c, val, slab):
        if slab:
            for j, a in enumerate(_lane_tiles(val)):
                ref[c // LANES + j] = a
        else:
            ref[:, c:c + val.shape[1]] = val.astype(bf16)

    kw = n_kv * HEAD_DIM
    col = 0
    oi = 0
    for g, slab in enumerate(slabbed):
        q_ref, k_ref, ksw_ref, v_ref, vsw_ref = out_refs[oi:oi + 5]
        oi += 5
        for c in range(0, BRANCH, WIDE):
            zz = jnp.dot(h, w_ref[:, col + c:col + c + WIDE], preferred_element_type=f32)
            for cc in range(0, WIDE, CHUNK):
                put(q_ref, c + cc, head_norm_rope(zz[:, cc:cc + CHUNK], gq_ref[g]), slab)
        col += BRANCH
        zz = jnp.dot(h, w_ref[:, col:col + 2 * kw], preferred_element_type=f32)
        k = head_norm_rope(zz[:, :kw], gk_ref[g])
        put(k_ref, 0, k, slab)
        put(ksw_ref, 0, roll_tiles(k, LANES // 2), slab)
        v = zz[:, kw:]
        put(v_ref, 0, v, slab)
        put(vsw_ref, 0, roll_tiles(v, LANES // 2), slab)
        col += 2 * kw
    sg_ref = out_refs[oi]
    for c in range(0, BRANCH, WIDE):
        gt = jnp.dot(h, w_ref[:, col + c:col + c + WIDE], preferred_element_type=f32)
        sg_ref[:, c:c + WIDE] = (gt * (1.0 / (1.0 + jnp.exp(-gt)))).astype(bf16)


def _project(x2, norm_g, w, bd, rope, gq, gk, *, slabbed, n_kv, seq):
    n = x2.shape[0]
    cols = w.shape[1]
    kw = n_kv * HEAD_DIM
    t = PROJ_ROWS
    per_seq = seq // t
    row_spec = lambda wd: pl.BlockSpec((t, wd), lambda i: (i, 0))
    slab_spec = lambda wd: pl.BlockSpec((wd // LANES, t, LANES), lambda i: (0, i, 0))
    rope_spec = pl.BlockSpec((t, CHUNK), lambda i: (i % per_seq, 0))
    out_shapes, out_specs = [], []
    for slab in slabbed:
        for wd in (BRANCH, kw, kw, kw, kw):
            if slab:
                out_shapes.append(jax.ShapeDtypeStruct((wd // LANES, n, LANES), f32))
                out_specs.append(slab_spec(wd))
            else:
                out_shapes.append(jax.ShapeDtypeStruct((n, wd), bf16))
                out_specs.append(row_spec(wd))
    out_shapes.append(jax.ShapeDtypeStruct((n, BRANCH), bf16))
    out_specs.append(row_spec(BRANCH))
    return pl.pallas_call(
        functools.partial(_proj_kernel, slabbed=slabbed, n_kv=n_kv),
        grid=(n // t,),
        in_specs=[row_spec(D_MODEL), _const_spec((1, D_MODEL)), _const_spec((D_MODEL, cols)),
                  _const_spec((CHUNK, CHUNK)), rope_spec, rope_spec, rope_spec,
                  _const_spec(gq.shape), _const_spec(gk.shape)],
        out_specs=out_specs,
        out_shape=out_shapes,
        compiler_params=pltpu.CompilerParams(
            dimension_semantics=("parallel",), vmem_limit_bytes=VMEM_LIMIT),
        name="proj",
    )(x2, norm_g, w, bd, *rope, gq, gk)


def _slab_attention(q_tiles, k2, ks2, v2, vs2, eye, bias_t, mlo, mhi, sinks):
    npair = len(q_tiles) // 2
    lane_lo = lax.broadcasted_iota(jnp.int32, (BLOCK, LANES), 1) < HEAD_DIM
    variants = ((k2 * mlo, ks2 * mhi, v2 * mlo, vs2 * mhi),
                (ks2 * mlo, k2 * mhi, vs2 * mlo, v2 * mhi))
    results = []
    for kv in range(2):
        k_lo, k_hi, v_lo, v_hi = variants[kv]
        kx = (jnp.concatenate([k_lo, bias_t], axis=1), jnp.concatenate([k_hi, bias_t], axis=1))
        vx = (jnp.concatenate([v_lo, mlo], axis=1), jnp.concatenate([v_hi, mhi], axis=1))
        tiles = q_tiles[kv * npair:(kv + 1) * npair]
        qs = jnp.concatenate([jnp.concatenate([t, eye], axis=1) for t in tiles], axis=0)
        acc = None
        m_par, x_par = [], []
        for e in range(2):
            s_all = lax.dot_general(qs, kx[e], (((1,), (1,)), ((), ())), preferred_element_type=f32)
            ps, ms, xs = [], [], []
            for i in range(npair):
                s = s_all[BLOCK * i:BLOCK * (i + 1)]
                m = jnp.max(s, axis=-1, keepdims=True)
                if sinks is not None:
                    sk = sinks[kv * npair + i][e]
                    m = jnp.maximum(m, sk)
                    xs.append(jnp.exp2(sk - m))
                ps.append(jnp.exp2(s - m).astype(bf16))
                ms.append(m)
            ol = jnp.dot(jnp.concatenate(ps, axis=0), vx[e], preferred_element_type=f32)
            acc = ol if acc is None else acc + ol
            m_par.append(ms)
            x_par.append(xs)
        for i in range(npair):
            a = acc[BLOCK * i:BLOCK * (i + 1)]
            den = a[:, LANES:]
            if sinks is not None:
                den = den + jnp.where(lane_lo, x_par[0][i], x_par[1][i])
            o = a[:, :LANES] / den
            lse2 = jnp.where(lane_lo, m_par[0][i], m_par[1][i]) + jnp.log2(den)
            results.append((o, lse2))
    return results


def _compact_lse(lses):
    lane = lax.broadcasted_iota(jnp.int32, (BLOCK, LANES), 1) % HEAD_DIM
    out = jnp.zeros((BLOCK, LANES), f32)
    for j, l in enumerate(lses):
        out = jnp.where(lane == j, l, out)
    return out


def _attn_dense_kernel(*refs, n_kv, gated):
    (q_ref, kp_ref, kc_ref, ksp_ref, ksc_ref, vp_ref, vc_ref, vsp_ref, vsc_ref,
     eye_ref, bias_ref, mlo_ref, mhi_ref) = refs[:13]
    if gated:
        sink_ref, sg_ref, y_ref = refs[13:16]
        scratch = refs[16:]
    else:
        o_ref, lse_ref = refs[13:15]
        scratch = refs[15:]
    kf, ksf, vf, vsf = scratch
    n_slab = n_kv // 2
    tiles_per_slab = Q_SLABS // n_slab
    for full, p_ref, c_ref in ((kf, kp_ref, kc_ref), (ksf, ksp_ref, ksc_ref),
                               (vf, vp_ref, vc_ref), (vsf, vsp_ref, vsc_ref)):
        full[0:BLOCK] = p_ref[...]
        full[BLOCK:BLOCK + ATT_ROWS] = c_ref[...]
    eye, mlo, mhi = eye_ref[...], mlo_ref[...], mhi_ref[...]
    first_step = pl.program_id(1) == 0

    def block(i, carry):
        r0 = pl.multiple_of(i * BLOCK, BLOCK)
        rq = pl.ds(r0, BLOCK)
        rk = pl.ds(r0, 2 * BLOCK)
        bias_t = bias_ref[jnp.where(jnp.logical_and(first_step, i == 0), 0, 1)]
        for sl in range(n_slab):
            ls = slice(LANES * sl, LANES * (sl + 1))
            t0 = sl * tiles_per_slab
            q_tiles = [q_ref[rq, LANES * j:LANES * (j + 1)] for j in range(t0, t0 + tiles_per_slab)]
            sinks = None
            if gated:
                sinks = [(sink_ref[2 * j], sink_ref[2 * j + 1]) for j in range(t0, t0 + tiles_per_slab)]
            res = _slab_attention(q_tiles, kf[rk, ls], ksf[rk, ls], vf[rk, ls], vsf[rk, ls],
                                  eye, bias_t, mlo, mhi, sinks)
            for j, (o, lse2) in enumerate(res):
                cs = slice(LANES * (t0 + j), LANES * (t0 + j + 1))
                if gated:
                    y_ref[rq, cs] = (o * sg_ref[rq, cs].astype(f32)).astype(bf16)
                else:
                    o_ref[rq, cs] = o.astype(bf16)
            if not gated:
                lse_ref[sl, rq, :] = _compact_lse([l for _, l in res])
        return carry

    lax.fori_loop(0, ATT_ROWS // BLOCK, block, 0, unroll=True)


def _attention_dense(q, k, ksw, v, vsw, consts, *, batch, seq, n_kv, sinks=None, sg=None):
    n = batch * seq
    kw = n_kv * HEAD_DIM
    gated = sinks is not None
    per_seq = seq // ATT_ROWS
    ratio = ATT_ROWS // BLOCK
    cur = lambda wd: pl.BlockSpec((ATT_ROWS, wd), lambda b, j: (b * per_seq + j, 0))
    prev = lambda wd: pl.BlockSpec(
        (BLOCK, wd), lambda b, j: ((b * per_seq + j) * ratio - jnp.minimum(j, 1), 0))
    eye, bias, mlo, mhi = consts
    in_specs = [cur(BRANCH), prev(kw), cur(kw), prev(kw), cur(kw), prev(kw), cur(kw), prev(kw), cur(kw),
                _const_spec(eye.shape), _const_spec(bias.shape), _const_spec(mlo.shape),
                _const_spec(mhi.shape)]
    args = [q, k, k, ksw, ksw, v, v, vsw, vsw, eye, bias, mlo, mhi]
    if gated:
        in_specs += [pl.BlockSpec(memory_space=pltpu.SMEM), cur(BRANCH)]
        args += [sinks, sg]
        out_shape = jax.ShapeDtypeStruct((n, BRANCH), bf16)
        out_specs = cur(BRANCH)
    else:
        out_shape = [jax.ShapeDtypeStruct((n, BRANCH), bf16),
                     jax.ShapeDtypeStruct((n_kv // 2, n, LANES), f32)]
        out_specs = [cur(BRANCH),
                     pl.BlockSpec((n_kv // 2, ATT_ROWS, LANES), lambda b, j: (0, b * per_seq + j, 0))]
    return pl.pallas_call(
        functools.partial(_attn_dense_kernel, n_kv=n_kv, gated=gated),
        grid=(batch, per_seq),
        in_specs=in_specs,
        out_specs=out_specs,
        out_shape=out_shape,
        scratch_shapes=[pltpu.VMEM((BLOCK + ATT_ROWS, kw), bf16)] * 4,
        compiler_params=pltpu.CompilerParams(
            dimension_semantics=("parallel", "arbitrary"), vmem_limit_bytes=VMEM_LIMIT),
        name="attn_dense_a" if gated else "attn_dense_b",
    )(*args)


def _attn_strided_kernel(q_ref, kp_ref, kc_ref, ksp_ref, ksc_ref, vp_ref, vc_ref, vsp_ref, vsc_ref,
                         eye_ref, bias_ref, mlo_ref, mhi_ref, o_ref, lse_ref, *, dilation):
    eye, mlo, mhi = eye_ref[...], mlo_ref[...], mhi_ref[...]
    bias_t = bias_ref[jnp.minimum(pl.program_id(1), 1)]
    n_tiles = q_ref.shape[0]

    def residue(c, carry):
        rows = pl.ds(c, BLOCK, stride=dilation)
        both = lambda p_ref, c_ref: jnp.concatenate([p_ref[0, rows, :], c_ref[0, rows, :]], axis=0).astype(bf16)
        q_tiles = [q_ref[j, rows, :].astype(bf16) for j in range(n_tiles)]
        res = _slab_attention(q_tiles, both(kp_ref, kc_ref), both(ksp_ref, ksc_ref),
                              both(vp_ref, vc_ref), both(vsp_ref, vsc_ref), eye, bias_t, mlo, mhi, None)
        for j, (o, _) in enumerate(res):
            o_ref[j, rows, :] = o
        lse_ref[0, rows, :] = _compact_lse([l for _, l in res])
        return carry

    lax.fori_loop(0, dilation, residue, 0, unroll=4)


def _attention_strided(q, k, ksw, v, vsw, consts, *, batch, seq, dilation, n_kv):
    n = batch * seq
    n_slab = n_kv // 2
    rows = BLOCK * dilation
    per_seq = seq // rows
    tiles = Q_SLABS // n_slab
    cur = lambda nt: pl.BlockSpec((nt, rows, LANES), lambda b, j, s: (s, b * per_seq + j, 0))
    prev = lambda nt: pl.BlockSpec((nt, rows, LANES), lambda b, j, s: (s, b * per_seq + jnp.maximum(j - 1, 0), 0))
    eye, bias, mlo, mhi = consts
    return pl.pallas_call(
        functools.partial(_attn_strided_kernel, dilation=dilation),
        grid=(batch, per_seq, n_slab),
        in_specs=[cur(tiles), prev(1), cur(1), prev(1), cur(1), prev(1), cur(1), prev(1), cur(1),
                  _const_spec(eye.shape), _const_spec(bias.shape), _const_spec(mlo.shape),
                  _const_spec(mhi.shape)],
        out_specs=[cur(tiles), cur(1)],
        out_shape=[jax.ShapeDtypeStruct((Q_SLABS, n, LANES), f32),
                   jax.ShapeDtypeStruct((n_slab, n, LANES), f32)],
        compiler_params=pltpu.CompilerParams(
            dimension_semantics=("parallel", "arbitrary", "arbitrary"), vmem_limit_bytes=VMEM_LIMIT),
        name=f"attn_strided_d{dilation}",
    )(q, k, k, ksw, ksw, v, v, vsw, vsw, eye, bias, mlo, mhi)


def _out_kernel_a(x_ref, y_ref, w_ref, o_ref):
    y = y_ref[...]
    for c in range(0, D_MODEL, CHUNK):
        o_ref[:, c:c + CHUNK] = x_ref[:, c:c + CHUNK] + jnp.dot(
            y, w_ref[:, c:c + CHUNK], preferred_element_type=f32)


def _out_kernel_b(x_ref, o0_ref, l0_ref, o1_ref, l1_ref, o2_ref, l2_ref, sg_ref, ex_ref, w_ref, o_ref):
    n_slab = l0_ref.shape[0]
    half = BRANCH // n_slab
    ex = ex_ref[...]
    wts = [[], [], []]
    for s in range(n_slab):
        l0, l1, l2 = l0_ref[s], l1_ref[s], l2_ref[s]
        mx = jnp.maximum(jnp.maximum(l0, l1), l2)
        e = [jnp.exp2(l0 - mx), jnp.exp2(l1 - mx), jnp.exp2(l2 - mx)]
        inv = 1.0 / (e[0] + e[1] + e[2])
        for g in range(3):
            wg = e[g] * inv
            hi = wg.astype(bf16)
            lo = (wg - hi.astype(f32)).astype(bf16)
            wts[g].append(jnp.dot(jnp.concatenate([hi, lo], axis=1), ex, preferred_element_type=f32))
    w0, w1, w2 = (jnp.concatenate(w, axis=1) for w in wts)
    o1 = jnp.concatenate([o1_ref[j] for j in range(Q_SLABS)], axis=1)
    o2 = jnp.concatenate([o2_ref[j] for j in range(Q_SLABS)], axis=1)
    y = ((w0 * o0_ref[...].astype(f32) + w1 * o1 + w2 * o2) * sg_ref[...].astype(f32)).astype(bf16)
    for c in range(0, D_MODEL, CHUNK):
        o_ref[:, c:c + CHUNK] = x_ref[:, c:c + CHUNK] + jnp.dot(
            y, w_ref[:, c:c + CHUNK], preferred_element_type=f32)


def _out_project(x2, w, *branch_inputs):
    n = x2.shape[0]
    t = OUT_ROWS
    row = pl.BlockSpec((t, D_MODEL), lambda i: (i, 0))
    slab = lambda a: pl.BlockSpec((a.shape[0], t, LANES), lambda i: (0, i, 0))
    if len(branch_inputs) == 1:
        kern, specs = _out_kernel_a, [row]
    else:
        kern = _out_kernel_b
        specs = [row if a.ndim == 2 and a.shape[0] == n else
                 (slab(a) if a.ndim == 3 else _const_spec(a.shape)) for a in branch_inputs]
    return pl.pallas_call(
        kern,
        grid=(n // t,),
        in_specs=[row] + specs + [_const_spec(w.shape)],
        out_specs=row,
        out_shape=jax.ShapeDtypeStruct((n, D_MODEL), f32),
        input_output_aliases={0: 0},
        compiler_params=pltpu.CompilerParams(
            dimension_semantics=("parallel",), vmem_limit_bytes=VMEM_LIMIT),
        name="outproj",
    )(x2, *branch_inputs, w)


def _rope_tables(seq):
    pos = jnp.arange(seq, dtype=f32)
    inv = ROPE_THETA ** (-jnp.arange(0, ROT_DIM, 2, dtype=f32) / ROT_DIM)
    ang = pos[:, None] * inv[None, :]
    cos, sin = jnp.cos(ang), jnp.sin(ang)
    half = ROT_DIM // 2
    pad = HEAD_DIM - ROT_DIM
    one = jnp.ones((seq, pad), f32)
    zero = jnp.zeros((seq, pad), f32)
    zh = jnp.zeros((seq, half), f32)
    c = jnp.concatenate([cos, cos, one], axis=1)
    s1 = jnp.concatenate([-sin, zh, zero], axis=1)
    s2 = jnp.concatenate([zh, sin, zero], axis=1)
    tile = lambda a: jnp.tile(a, (1, CHUNK // HEAD_DIM))
    return tile(c), tile(s1), tile(s2)


def _bias_tables(max_dist):
    i = jnp.arange(BLOCK)[None, :]
    j = jnp.arange(2 * BLOCK)[:, None]
    dist = BLOCK + i - j
    ok = (dist >= 0) & (dist <= max_dist)
    first = ok & (j >= BLOCK)
    return jnp.stack([jnp.where(first, 0.0, NEG), jnp.where(ok, 0.0, NEG)]).astype(bf16)


def kernel(x, norm_a, w_in_a, q_gain_a, k_gain_a, sinks_a, w_out_a,
           norm_b, w_in_b, q_gain_b, k_gain_b, w_out_b):
    batch, seq, _ = x.shape
    n = batch * seq
    rope = _rope_tables(seq)
    hid = jnp.arange(CHUNK) // HEAD_DIM
    bd = jnp.where(hid[:, None] == hid[None, :], 1.0 / HEAD_DIM, 0.0).astype(bf16)
    lane = jnp.arange(LANES)[None, :]
    mlo = jnp.broadcast_to(lane < HEAD_DIM, (2 * BLOCK, LANES)).astype(bf16)
    mhi = jnp.broadcast_to(lane >= HEAD_DIM, (2 * BLOCK, LANES)).astype(bf16)
    eye = jnp.eye(BLOCK, dtype=bf16)
    consts_a = (eye, _bias_tables(A_WINDOW - 1), mlo, mhi)
    consts_b = (eye, _bias_tables(BLOCK), mlo, mhi)
    half = BRANCH // (B_KV // 2)
    col_tile = jnp.arange(half)[None, :] // LANES
    col_odd = (jnp.arange(half)[None, :] % LANES) // HEAD_DIM
    row = jnp.arange(LANES)[:, None]
    ex = ((row % HEAD_DIM == col_tile) & (row // HEAD_DIM == col_odd)).astype(bf16)
    ex = jnp.concatenate([ex, ex], axis=0)
    tile_gain = lambda g: jnp.tile(g, (1, CHUNK // HEAD_DIM))[:, None, :]
    q_scale = SCALE * LOG2E

    x2 = x.reshape(n, D_MODEL)
    depth = norm_a.shape[0] + norm_b.shape[0]
    for layer in range(depth):
        idx = layer // 2
        if layer % 2 == 0:
            q, k, ksw, v, vsw, sg = _project(
                x2, norm_a[idx][None], w_in_a[idx].astype(bf16), bd, rope,
                tile_gain(q_gain_a[idx][None] * q_scale), tile_gain(k_gain_a[idx][None]),
                slabbed=(False,), n_kv=A_KV, seq=seq)
            y = _attention_dense(q, k, ksw, v, vsw, consts_a, batch=batch, seq=seq, n_kv=A_KV,
                                 sinks=sinks_a[idx].astype(f32) * LOG2E, sg=sg)
            x2 = _out_project(x2, w_out_a[idx].astype(bf16), y)
        else:
            slabbed = tuple(d > 1 for _, d in B_PAIRS)
            outs = _project(
                x2, norm_b[idx][None], w_in_b[idx].astype(bf16), bd, rope,
                tile_gain(q_gain_b[idx] * q_scale), tile_gain(k_gain_b[idx]),
                slabbed=slabbed, n_kv=B_KV, seq=seq)
            sg = outs[-1]
            merged = []
            for g, (window, dilation) in enumerate(B_PAIRS):
                assert window // dilation == BLOCK
                qkv = outs[5 * g:5 * g + 5]
                if dilation == 1:
                    merged += _attention_dense(*qkv, consts_b, batch=batch, seq=seq, n_kv=B_KV)
                else:
                    merged += _attention_strided(*qkv, consts_b, batch=batch, seq=seq,
                                                 dilation=dilation, n_kv=B_KV)
            x2 = _out_project(x2, w_out_b[idx].astype(bf16), *merged, sg, ex)
    return x2.reshape(batch, seq, D_MODEL)
```

```python
import functools
import math

import jax
import jax.numpy as jnp
from jax import lax
from jax.experimental import pallas as pl
from jax.experimental.pallas import tpu as pltpu

D_MODEL = 1024
HEAD_DIM = 64
N_HEADS = 16
BRANCH = N_HEADS * HEAD_DIM
A_KV = 2
B_KV = 4
B_PAIRS = ((128, 1), (512, 4), (2048, 16))
A_WINDOW = 128
BLOCK = 128
ROT_DIM = HEAD_DIM // 4
ROPE_THETA = 500000.0
EPS = 1e-6
SCALE = HEAD_DIM ** -0.5
LOG2E = math.log2(math.e)
LANES = 128
Q_TILES = BRANCH // LANES
NEG = -1e30

TILE = 512
ATT_ROWS = 512
CHUNK = 256
WIDE = 512
MAX_DIL = 16
RUN = TILE // MAX_DIL
N_RUNS = BLOCK // RUN
VMEM_LIMIT = 56 * 1024 * 1024

bf16 = jnp.bfloat16
f32 = jnp.float32


def _const_spec(shape):
    nd = len(shape)
    return pl.BlockSpec(shape, lambda *_: (0,) * nd, pipeline_mode=pl.Buffered(1))


def _lane_tiles(z):
    return [z[:, i:i + LANES] for i in range(0, z.shape[1], LANES)]


def _proj_kernel(*refs, permuted, n_kv):
    x_ref, ng_ref, w_ref, bd_ref, gq_ref, gk_ref = refs[:6]
    n_tab = 2 if any(permuted) else 1
    tabs = [tuple(r[...] for r in refs[6 + 3 * i:9 + 3 * i]) for i in range(n_tab)]
    pos = 6 + 3 * n_tab
    if any(permuted):
        perm_ref = refs[pos]
        pos += 1
    out_refs = refs[pos:]

    x = x_ref[...]
    ms = jnp.mean(x * x, axis=-1, keepdims=True)
    h = ((x * lax.rsqrt(ms + EPS)) * ng_ref[...]).astype(bf16)
    hs = [h]
    if any(permuted):
        hs.append(jnp.dot(perm_ref[...], h, preferred_element_type=f32).astype(bf16))
    bd = bd_ref[...]

    def roll_tiles(t, shift):
        return jnp.concatenate([pltpu.roll(a, shift, axis=1) for a in _lane_tiles(t)], axis=1)

    def head_norm_rope(z, gain, tab):
        w = z.shape[1]
        cos, sn1, sn2 = tab
        msq = jnp.dot((z * z).astype(bf16), bd[:w, :w], preferred_element_type=f32)
        t = (z * lax.rsqrt(msq + EPS)) * gain[:, :w]
        up = roll_tiles(t, LANES - ROT_DIM // 2)
        dn = roll_tiles(t, ROT_DIM // 2)
        return t * cos[:, :w] + up * sn1[:, :w] + dn * sn2[:, :w]

    kw = n_kv * HEAD_DIM
    col = 0
    oi = 0
    for g, perm in enumerate(permuted):
        hg, tab = hs[int(perm)], tabs[int(perm)]
        q_ref, k_ref, ksw_ref, v_ref, vsw_ref = out_refs[oi:oi + 5]
        oi += 5
        for c in range(0, BRANCH, WIDE):
            zz = jnp.dot(hg, w_ref[:, col + c:col + c + WIDE], preferred_element_type=f32)
            for cc in range(0, WIDE, CHUNK):
                q_ref[:, c + cc:c + cc + CHUNK] = head_norm_rope(
                    zz[:, cc:cc + CHUNK], gq_ref[g], tab).astype(bf16)
        col += BRANCH
        zz = jnp.dot(hg, w_ref[:, col:col + 2 * kw], preferred_element_type=f32)
        k = head_norm_rope(zz[:, :kw], gk_ref[g], tab)
        k_ref[...] = k.astype(bf16)
        ksw_ref[...] = roll_tiles(k, LANES // 2).astype(bf16)
        v = zz[:, kw:]
        v_ref[...] = v.astype(bf16)
        vsw_ref[...] = roll_tiles(v, LANES // 2).astype(bf16)
        col += 2 * kw
    sg_ref = out_refs[oi]
    for c in range(0, BRANCH, WIDE):
        gt = jnp.dot(h, w_ref[:, col + c:col + c + WIDE], preferred_element_type=f32)
        sg_ref[:, c:c + WIDE] = (gt * (1.0 / (1.0 + jnp.exp(-gt)))).astype(bf16)


def _project(x2, norm_g, w, bd, gq, gk, rope_tabs, perm, *, permuted, n_kv, seq):
    n = x2.shape[0]
    cols = w.shape[1]
    kw = n_kv * HEAD_DIM
    per_seq = seq // TILE
    row_spec = lambda wd: pl.BlockSpec((TILE, wd), lambda i: (i, 0))
    rope_spec = pl.BlockSpec((TILE, CHUNK), lambda i: (i % per_seq, 0))
    in_specs = [row_spec(D_MODEL), _const_spec((1, D_MODEL)), _const_spec((D_MODEL, cols)),
                _const_spec((CHUNK, CHUNK)), _const_spec(gq.shape), _const_spec(gk.shape)]
    args = [x2, norm_g, w, bd, gq, gk]
    for tab in rope_tabs[:2 if any(permuted) else 1]:
        in_specs += [rope_spec] * 3
        args += list(tab)
    if any(permuted):
        in_specs.append(_const_spec(perm.shape))
        args.append(perm)
    out_shapes, out_specs = [], []
    for _ in permuted:
        for wd in (BRANCH, kw, kw, kw, kw):
            out_shapes.append(jax.ShapeDtypeStruct((n, wd), bf16))
            out_specs.append(row_spec(wd))
    out_shapes.append(jax.ShapeDtypeStruct((n, BRANCH), bf16))
    out_specs.append(row_spec(BRANCH))
    return pl.pallas_call(
        functools.partial(_proj_kernel, permuted=permuted, n_kv=n_kv),
        grid=(n // TILE,),
        in_specs=in_specs,
        out_specs=out_specs,
        out_shape=out_shapes,
        compiler_params=pltpu.CompilerParams(
            dimension_semantics=("parallel",), vmem_limit_bytes=VMEM_LIMIT),
        name="proj",
    )(*args)


def _slab_attention(q_tiles, k2, ks2, v2, vs2, eye, bias_t, mlo, mhi, sinks):
    npair = len(q_tiles) // 2
    lane_lo = lax.broadcasted_iota(jnp.int32, (BLOCK, LANES), 1) < HEAD_DIM
    variants = ((k2 * mlo, ks2 * mhi, v2 * mlo, vs2 * mhi),
                (ks2 * mlo, k2 * mhi, vs2 * mlo, v2 * mhi))
    results = []
    for kv in range(2):
        k_lo, k_hi, v_lo, v_hi = variants[kv]
        kx = (jnp.concatenate([k_lo, bias_t], axis=1), jnp.concatenate([k_hi, bias_t], axis=1))
        vx = (jnp.concatenate([v_lo, mlo], axis=1), jnp.concatenate([v_hi, mhi], axis=1))
        tiles = q_tiles[kv * npair:(kv + 1) * npair]
        qs = jnp.concatenate([jnp.concatenate([t, eye], axis=1) for t in tiles], axis=0)
        acc = None
        m_par, x_par = [], []
        for e in range(2):
            s_all = lax.dot_general(qs, kx[e], (((1,), (1,)), ((), ())), preferred_element_type=f32)
            ps, ms, xs = [], [], []
            for i in range(npair):
                s = s_all[BLOCK * i:BLOCK * (i + 1)]
                m = jnp.max(s, axis=-1, keepdims=True)
                if sinks is not None:
                    sk = sinks[kv * npair + i][e]
                    m = jnp.maximum(m, sk)
                    xs.append(jnp.exp2(sk - m))
                ps.append(jnp.exp2(s - m).astype(bf16))
                ms.append(m)
            ol = jnp.dot(jnp.concatenate(ps, axis=0), vx[e], preferred_element_type=f32)
            acc = ol if acc is None else acc + ol
            m_par.append(ms)
            x_par.append(xs)
        for i in range(npair):
            a = acc[BLOCK * i:BLOCK * (i + 1)]
            den = a[:, LANES:]
            if sinks is not None:
                den = den + jnp.where(lane_lo, x_par[0][i], x_par[1][i])
            o = a[:, :LANES] / den
            lse2 = jnp.where(lane_lo, m_par[0][i], m_par[1][i]) + jnp.log2(den)
            results.append((o, lse2))
    return results


def _compact_lse(lses):
    lane = lax.broadcasted_iota(jnp.int32, (BLOCK, LANES), 1) % HEAD_DIM
    out = jnp.zeros((BLOCK, LANES), f32)
    for j, l in enumerate(lses):
        out = jnp.where(lane == j, l, out)
    return out


def _attn_dense_kernel(*refs, n_kv, gated):
    (q_ref, kp_ref, kc_ref, ksp_ref, ksc_ref, vp_ref, vc_ref, vsp_ref, vsc_ref,
     eye_ref, bias_ref, mlo_ref, mhi_ref) = refs[:13]
    if gated:
        sink_ref, sg_ref, x_ref, wo_ref, xo_ref, kf, ksf, vf, vsf, y_buf = refs[13:]
    else:
        o_ref, lse_ref, kf, ksf, vf, vsf = refs[13:]
    n_slab = n_kv // 2
    tiles_per_slab = Q_TILES // n_slab
    for full, p_ref, c_ref in ((kf, kp_ref, kc_ref), (ksf, ksp_ref, ksc_ref),
                               (vf, vp_ref, vc_ref), (vsf, vsp_ref, vsc_ref)):
        full[0:BLOCK] = p_ref[...]
        full[BLOCK:BLOCK + ATT_ROWS] = c_ref[...]
    eye, mlo, mhi = eye_ref[...], mlo_ref[...], mhi_ref[...]
    first_step = pl.program_id(1) == 0

    def block(i, carry):
        r0 = pl.multiple_of(i * BLOCK, BLOCK)
        rq = pl.ds(r0, BLOCK)
        rk = pl.ds(r0, 2 * BLOCK)
        bias_t = bias_ref[jnp.where(jnp.logical_and(first_step, i == 0), 0, 1)]
        for sl in range(n_slab):
            ls = slice(LANES * sl, LANES * (sl + 1))
            t0 = sl * tiles_per_slab
            q_tiles = [q_ref[rq, LANES * j:LANES * (j + 1)] for j in range(t0, t0 + tiles_per_slab)]
            sinks = None
            if gated:
                sinks = [(sink_ref[2 * j], sink_ref[2 * j + 1]) for j in range(t0, t0 + tiles_per_slab)]
            res = _slab_attention(q_tiles, kf[rk, ls], ksf[rk, ls], vf[rk, ls], vsf[rk, ls],
                                  eye, bias_t, mlo, mhi, sinks)
            for j, (o, lse2) in enumerate(res):
                cs = slice(LANES * (t0 + j), LANES * (t0 + j + 1))
                if gated:
                    y_buf[rq, cs] = (o * sg_ref[rq, cs].astype(f32)).astype(bf16)
                else:
                    o_ref[rq, cs] = o.astype(bf16)
            if not gated:
                lse_ref[sl, rq, :] = _compact_lse([l for _, l in res])
        return carry

    lax.fori_loop(0, ATT_ROWS // BLOCK, block, 0, unroll=True)
    if gated:
        y = y_buf[...]
        for c in range(0, D_MODEL, WIDE):
            xo_ref[:, c:c + WIDE] = x_ref[:, c:c + WIDE] + jnp.dot(
                y, wo_ref[:, c:c + WIDE], preferred_element_type=f32)


def _attention_dense(q, k, ksw, v, vsw, consts, *, batch, seq, n_kv, gate_args=None):
    n = batch * seq
    kw = n_kv * HEAD_DIM
    gated = gate_args is not None
    per_seq = seq // ATT_ROWS
    ratio = ATT_ROWS // BLOCK
    cur = lambda wd: pl.BlockSpec((ATT_ROWS, wd), lambda b, j: (b * per_seq + j, 0))
    prev = lambda wd: pl.BlockSpec(
        (BLOCK, wd), lambda b, j: ((b * per_seq + j) * ratio - jnp.minimum(j, 1), 0))
    eye, bias, mlo, mhi = consts
    in_specs = [cur(BRANCH), prev(kw), cur(kw), prev(kw), cur(kw), prev(kw), cur(kw), prev(kw), cur(kw),
                _const_spec(eye.shape), _const_spec(bias.shape), _const_spec(mlo.shape),
                _const_spec(mhi.shape)]
    args = [q, k, k, ksw, ksw, v, v, vsw, vsw, eye, bias, mlo, mhi]
    scratch = [pltpu.VMEM((BLOCK + ATT_ROWS, kw), bf16)] * 4
    if gated:
        sinks, sg, x2, w_out = gate_args
        in_specs += [pl.BlockSpec(memory_space=pltpu.SMEM), cur(BRANCH), cur(D_MODEL),
                     _const_spec(w_out.shape)]
        args += [sinks, sg, x2, w_out]
        out_shape = jax.ShapeDtypeStruct((n, D_MODEL), f32)
        out_specs = cur(D_MODEL)
        scratch = scratch + [pltpu.VMEM((ATT_ROWS, BRANCH), bf16)]
    else:
        out_shape = [jax.ShapeDtypeStruct((n, BRANCH), bf16),
                     jax.ShapeDtypeStruct((n_kv // 2, n, LANES), f32)]
        out_specs = [cur(BRANCH),
                     pl.BlockSpec((n_kv // 2, ATT_ROWS, LANES), lambda b, j: (0, b * per_seq + j, 0))]
    return pl.pallas_call(
        functools.partial(_attn_dense_kernel, n_kv=n_kv, gated=gated),
        grid=(batch, per_seq),
        in_specs=in_specs,
        out_specs=out_specs,
        out_shape=out_shape,
        scratch_shapes=scratch,
        compiler_params=pltpu.CompilerParams(
            dimension_semantics=("parallel", "arbitrary"), vmem_limit_bytes=VMEM_LIMIT),
        name="attn_dense_a" if gated else "attn_dense_b",
    )(*args)


def _attn_runs_kernel(q_ref, kp_ref, kc_ref, ksp_ref, ksc_ref, vp_ref, vc_ref, vsp_ref, vsc_ref,
                      eye_ref, bias_ref, mlo_ref, mhi_ref, o_ref, lse_ref, *, dilation, n_kv):
    eye, mlo, mhi = eye_ref[...], mlo_ref[...], mhi_ref[...]
    bias_t = bias_ref[jnp.minimum(pl.program_id(1), 1)]
    n_slab = n_kv // 2
    tiles_per_slab = Q_TILES // n_slab
    run_pitch = BLOCK * dilation // N_RUNS

    def residue(c, carry):
        runs = [pl.ds(pl.multiple_of(i * run_pitch + c * RUN, RUN), RUN) for i in range(N_RUNS)]
        gather = lambda ref, ls: jnp.concatenate([ref[r, ls] for r in runs], axis=0)
        both = lambda p_ref, c_ref, ls: jnp.concatenate([gather(p_ref, ls), gather(c_ref, ls)], axis=0)
        for sl in range(n_slab):
            ls = slice(LANES * sl, LANES * (sl + 1))
            t0 = sl * tiles_per_slab
            q_tiles = [gather(q_ref, slice(LANES * j, LANES * (j + 1))) for j in range(t0, t0 + tiles_per_slab)]
            res = _slab_attention(q_tiles, both(kp_ref, kc_ref, ls), both(ksp_ref, ksc_ref, ls),
                                  both(vp_ref, vc_ref, ls), both(vsp_ref, vsc_ref, ls),
                                  eye, bias_t, mlo, mhi, None)
            lse = _compact_lse([l for _, l in res])
            for i, r in enumerate(runs):
                rr = slice(RUN * i, RUN * (i + 1))
                for j, (o, _) in enumerate(res):
                    o_ref[r, LANES * (t0 + j):LANES * (t0 + j + 1)] = o[rr].astype(bf16)
                lse_ref[sl, r, :] = lse[rr]
        return carry

    lax.fori_loop(0, dilation, residue, 0, unroll=4)


def _attention_runs(q, k, ksw, v, vsw, consts, *, batch, seq, dilation, n_kv):
    n = batch * seq
    kw = n_kv * HEAD_DIM
    rows = BLOCK * dilation
    per_seq = seq // rows
    cur = lambda wd: pl.BlockSpec((rows, wd), lambda b, j: (b * per_seq + j, 0))
    prev = lambda wd: pl.BlockSpec((rows, wd), lambda b, j: (b * per_seq + jnp.maximum(j - 1, 0), 0))
    eye, bias, mlo, mhi = consts
    return pl.pallas_call(
        functools.partial(_attn_runs_kernel, dilation=dilation, n_kv=n_kv),
        grid=(batch, per_seq),
        in_specs=[cur(BRANCH), prev(kw), cur(kw), prev(kw), cur(kw), prev(kw), cur(kw), prev(kw), cur(kw),
                  _const_spec(eye.shape), _const_spec(bias.shape), _const_spec(mlo.shape),
                  _const_spec(mhi.shape)],
        out_specs=[cur(BRANCH),
                   pl.BlockSpec((n_kv // 2, rows, LANES), lambda b, j: (0, b * per_seq + j, 0))],
        out_shape=[jax.ShapeDtypeStruct((n, BRANCH), bf16),
                   jax.ShapeDtypeStruct((n_kv // 2, n, LANES), f32)],
        compiler_params=pltpu.CompilerParams(
            dimension_semantics=("parallel", "arbitrary"), vmem_limit_bytes=VMEM_LIMIT),
        name=f"attn_runs_d{dilation}",
    )(q, k, k, ksw, ksw, v, v, vsw, vsw, eye, bias, mlo, mhi)


def _merge_kernel(x_ref, o0_ref, l0_ref, o1_ref, l1_ref, o2_ref, l2_ref, sg_ref, ex_ref, unperm_ref,
                  w_ref, o_ref, lnat):
    n_slab = l0_ref.shape[0]
    o12 = jnp.dot(unperm_ref[...], jnp.concatenate([o1_ref[...], o2_ref[...]], axis=1),
                  preferred_element_type=f32)
    for gi, l_ref in enumerate((l1_ref, l2_ref)):
        for s in range(n_slab):
            for c in range(MAX_DIL):
                lnat[gi * n_slab + s, pl.ds(c, RUN, stride=MAX_DIL), :] = l_ref[s, RUN * c:RUN * (c + 1), :]
    ex = ex_ref[...]
    wts = [[], [], []]
    for s in range(n_slab):
        l0, l1, l2 = l0_ref[s], lnat[s], lnat[n_slab + s]
        mx = jnp.maximum(jnp.maximum(l0, l1), l2)
        e = [jnp.exp2(l0 - mx), jnp.exp2(l1 - mx), jnp.exp2(l2 - mx)]
        inv = 1.0 / (e[0] + e[1] + e[2])
        for g in range(3):
            wg = e[g] * inv
            hi = wg.astype(bf16)
            lo = (wg - hi.astype(f32)).astype(bf16)
            wts[g].append(jnp.dot(jnp.concatenate([hi, lo], axis=1), ex, preferred_element_type=f32))
    w0, w1, w2 = (jnp.concatenate(w, axis=1) for w in wts)
    y = ((w0 * o0_ref[...].astype(f32) + w1 * o12[:, :BRANCH] + w2 * o12[:, BRANCH:])
         * sg_ref[...].astype(f32)).astype(bf16)
    for c in range(0, D_MODEL, WIDE):
        o_ref[:, c:c + WIDE] = x_ref[:, c:c + WIDE] + jnp.dot(
            y, w_ref[:, c:c + WIDE], preferred_element_type=f32)


def _merge_project(x2, o0, l0, o1, l1, o2, l2, sg, ex, unperm, w):
    n = x2.shape[0]
    row = pl.BlockSpec((TILE, D_MODEL), lambda i: (i, 0))
    slab = pl.BlockSpec((l0.shape[0], TILE, LANES), lambda i: (0, i, 0))
    return pl.pallas_call(
        _merge_kernel,
        grid=(n // TILE,),
        in_specs=[row, row, slab, row, slab, row, slab, row, _const_spec(ex.shape),
                  _const_spec(unperm.shape), _const_spec(w.shape)],
        out_specs=row,
        out_shape=jax.ShapeDtypeStruct((n, D_MODEL), f32),
        scratch_shapes=[pltpu.VMEM((2 * l0.shape[0], TILE, LANES), f32)],
        input_output_aliases={0: 0},
        compiler_params=pltpu.CompilerParams(
            dimension_semantics=("parallel",), vmem_limit_bytes=VMEM_LIMIT),
        name="merge_outproj",
    )(x2, o0, l0, o1, l1, o2, l2, sg, ex, unperm, w)


def _rope_tables(seq):
    pos = jnp.arange(seq, dtype=f32)
    inv = ROPE_THETA ** (-jnp.arange(0, ROT_DIM, 2, dtype=f32) / ROT_DIM)
    ang = pos[:, None] * inv[None, :]
    cos, sin = jnp.cos(ang), jnp.sin(ang)
    half = ROT_DIM // 2
    pad = HEAD_DIM - ROT_DIM
    one = jnp.ones((seq, pad), f32)
    zero = jnp.zeros((seq, pad), f32)
    zh = jnp.zeros((seq, half), f32)
    c = jnp.concatenate([cos, cos, one], axis=1)
    s1 = jnp.concatenate([-sin, zh, zero], axis=1)
    s2 = jnp.concatenate([zh, sin, zero], axis=1)
    tile = lambda a: jnp.tile(a, (1, CHUNK // HEAD_DIM))
    return tile(c), tile(s1), tile(s2)


def _bias_tables(max_dist, order):
    qi = order[None, :]
    kj = jnp.concatenate([order, order + BLOCK])[:, None]
    dist = BLOCK + qi - kj
    ok = (dist >= 0) & (dist <= max_dist)
    first = ok & (kj >= BLOCK)
    return jnp.stack([jnp.where(first, 0.0, NEG), jnp.where(ok, 0.0, NEG)]).astype(bf16)


def kernel(x, norm_a, w_in_a, q_gain_a, k_gain_a, sinks_a, w_out_a,
           norm_b, w_in_b, q_gain_b, k_gain_b, w_out_b):
    batch, seq, _ = x.shape
    n = batch * seq
    hid = jnp.arange(CHUNK) // HEAD_DIM
    bd = jnp.where(hid[:, None] == hid[None, :], 1.0 / HEAD_DIM, 0.0).astype(bf16)
    lane = jnp.arange(LANES)[None, :]
    mlo = jnp.broadcast_to(lane < HEAD_DIM, (2 * BLOCK, LANES)).astype(bf16)
    mhi = jnp.broadcast_to(lane >= HEAD_DIM, (2 * BLOCK, LANES)).astype(bf16)
    eye = jnp.eye(BLOCK, dtype=bf16)
    natural = jnp.arange(BLOCK)
    consts_a = (eye, _bias_tables(A_WINDOW - 1, natural), mlo, mhi)

    r = jnp.arange(TILE)
    dest = RUN * (r % MAX_DIL) + r // MAX_DIL
    perm = (jnp.arange(TILE)[:, None] == dest[None, :]).astype(bf16)
    src = MAX_DIL * (r % RUN) + r // RUN
    rope_nat = _rope_tables(seq)
    rope_perm = tuple(t.reshape(seq // TILE, TILE, CHUNK)[:, src, :].reshape(seq, CHUNK) for t in rope_nat)
    order = {}
    for _, d in B_PAIRS:
        if d == 1 or d == MAX_DIL:
            order[d] = natural
        else:
            assert BLOCK * d == TILE and MAX_DIL % d == 0
            step = MAX_DIL // d
            order[d] = step * (natural % RUN) + natural // RUN
    consts_b = {d: (eye, _bias_tables(BLOCK, order[d]), mlo, mhi) for _, d in B_PAIRS}

    half = BRANCH // (B_KV // 2)
    col_tile = jnp.arange(half)[None, :] // LANES
    col_odd = (jnp.arange(half)[None, :] % LANES) // HEAD_DIM
    row = jnp.arange(LANES)[:, None]
    ex = ((row % HEAD_DIM == col_tile) & (row // HEAD_DIM == col_odd)).astype(bf16)
    ex = jnp.concatenate([ex, ex], axis=0)
    tile_gain = lambda g: jnp.tile(g, (1, CHUNK // HEAD_DIM))[:, None, :]
    q_scale = SCALE * LOG2E

    x2 = x.reshape(n, D_MODEL)
    depth = norm_a.shape[0] + norm_b.shape[0]
    for layer in range(depth):
        idx = layer // 2
        if layer % 2 == 0:
            q, k, ksw, v, vsw, sg = _project(
                x2, norm_a[idx][None], w_in_a[idx].astype(bf16), bd,
                tile_gain(q_gain_a[idx][None] * q_scale), tile_gain(k_gain_a[idx][None]),
                (rope_nat,), None, permuted=(False,), n_kv=A_KV, seq=seq)
            x2 = _attention_dense(
                q, k, ksw, v, vsw, consts_a, batch=batch, seq=seq, n_kv=A_KV,
                gate_args=(sinks_a[idx].astype(f32) * LOG2E, sg, x2, w_out_a[idx].astype(bf16)))
        else:
            permuted = tuple(d > 1 for _, d in B_PAIRS)
            outs = _project(
                x2, norm_b[idx][None], w_in_b[idx].astype(bf16), bd,
                tile_gain(q_gain_b[idx] * q_scale), tile_gain(k_gain_b[idx]),
                (rope_nat, rope_perm), perm, permuted=permuted, n_kv=B_KV, seq=seq)
            sg = outs[-1]
            merged = []
            for g, (window, dilation) in enumerate(B_PAIRS):
                assert window // dilation == BLOCK
                qkv = outs[5 * g:5 * g + 5]
                if dilation == 1:
                    merged += _attention_dense(*qkv, consts_b[1], batch=batch, seq=seq, n_kv=B_KV)
                else:
                    merged += _attention_runs(*qkv, consts_b[dilation], batch=batch, seq=seq,
                                              dilation=dilation, n_kv=B_KV)
            x2 = _merge_project(x2, *merged, sg, ex, perm.T, w_out_b[idx].astype(bf16))
    return x2.reshape(batch, seq, D_MODEL)
```

```python
import functools
import math

import jax
import jax.numpy as jnp
import numpy as np
from jax import lax
from jax.experimental import pallas as pl
from jax.experimental.pallas import tpu as pltpu

D_MODEL = 1024
HEAD_DIM = 64
N_HEADS = 16
BRANCH = N_HEADS * HEAD_DIM
A_KV = 2
B_KV = 4
B_PAIRS = ((128, 1), (512, 4), (2048, 16))
A_WINDOW = 128
BLOCK = 128
ROT_DIM = HEAD_DIM // 4
ROPE_THETA = 500000.0
EPS = 1e-6
SCALE = HEAD_DIM ** -0.5
LOG2E = math.log2(math.e)
LANES = 128
Q_TILES = BRANCH // LANES
NEG = -1e30

TILE = 512
ATT_ROWS = 1024
CHUNK = 256
WIDE = 512
MAX_DIL = 16
RUN = TILE // MAX_DIL
N_RUNS = BLOCK // RUN
VMEM_LIMIT = 56 * 1024 * 1024

bf16 = jnp.bfloat16
f32 = jnp.float32


def _const_spec(shape):
    nd = len(shape)
    return pl.BlockSpec(shape, lambda *_: (0,) * nd, pipeline_mode=pl.Buffered(1))


def _layer_spec(stacked_and_index):
    stacked, index = stacked_and_index
    return pl.BlockSpec((None,) + stacked.shape[1:], lambda *_: (index, 0, 0), pipeline_mode=pl.Buffered(1))


def _lane_tiles(z):
    return [z[:, i:i + LANES] for i in range(0, z.shape[1], LANES)]


def _proj_kernel(*refs, permuted, n_kv):
    x_ref, ng_ref, w_ref, bd_ref, gq_ref, gk_ref = refs[:6]
    n_tab = 2 if any(permuted) else 1
    tabs = [tuple(r[...] for r in refs[6 + 3 * i:9 + 3 * i]) for i in range(n_tab)]
    pos = 6 + 3 * n_tab
    if any(permuted):
        perm_ref = refs[pos]
        pos += 1
    out_refs = refs[pos:]

    x = x_ref[...]
    ms = jnp.mean(x * x, axis=-1, keepdims=True)
    h = ((x * lax.rsqrt(ms + EPS)) * ng_ref[...]).astype(bf16)
    hs = [h]
    if any(permuted):
        hs.append(jnp.dot(perm_ref[...], h, preferred_element_type=f32).astype(bf16))
    bd = bd_ref[...]

    def roll_tiles(t, shift):
        return jnp.concatenate([pltpu.roll(a, shift, axis=1) for a in _lane_tiles(t)], axis=1)

    def head_norm_rope(z, gain, tab):
        w = z.shape[1]
        cos, sn1, sn2 = (jnp.concatenate([a] * (w // LANES), axis=1) for a in tab)
        msq = jnp.dot((z * z).astype(bf16), bd[:w, :w], preferred_element_type=f32)
        t = (z * lax.rsqrt(msq + EPS)) * gain[:, :w]
        up = roll_tiles(t, LANES - ROT_DIM // 2)
        dn = roll_tiles(t, ROT_DIM // 2)
        return t * cos + up * sn1 + dn * sn2

    kw = n_kv * HEAD_DIM
    col = 0
    oi = 0
    for g, perm in enumerate(permuted):
        hg, tab = hs[int(perm)], tabs[int(perm)]
        q_ref, k_ref, ksw_ref, v_ref, vsw_ref = out_refs[oi:oi + 5]
        oi += 5
        for c in range(0, BRANCH, WIDE):
            zz = jnp.dot(hg, w_ref[:, col + c:col + c + WIDE], preferred_element_type=f32)
            for cc in range(0, WIDE, CHUNK):
                q_ref[:, c + cc:c + cc + CHUNK] = head_norm_rope(
                    zz[:, cc:cc + CHUNK], gq_ref[g], tab).astype(bf16)
        col += BRANCH
        zz = jnp.dot(hg, w_ref[:, col:col + 2 * kw], preferred_element_type=f32)
        k = head_norm_rope(zz[:, :kw], gk_ref[g], tab)
        k_ref[...] = k.astype(bf16)
        ksw_ref[...] = roll_tiles(k, LANES // 2).astype(bf16)
        v = zz[:, kw:]
        v_ref[...] = v.astype(bf16)
        vsw_ref[...] = roll_tiles(v, LANES // 2).astype(bf16)
        col += 2 * kw
    sg_ref = out_refs[oi]
    for c in range(0, BRANCH, WIDE):
        gt = jnp.dot(h, w_ref[:, col + c:col + c + WIDE], preferred_element_type=f32)
        sg_ref[:, c:c + WIDE] = (gt * (1.0 / (1.0 + jnp.exp(-gt)))).astype(bf16)


def _project(x2, norm_g, w, bd, gq, gk, rope_tabs, perm, *, permuted, n_kv, seq):
    n = x2.shape[0]
    kw = n_kv * HEAD_DIM
    per_seq = seq // TILE
    row_spec = lambda wd: pl.BlockSpec((TILE, wd), lambda i: (i, 0))
    rope_spec = pl.BlockSpec((TILE, LANES), lambda i: (i % per_seq, 0))
    in_specs = [row_spec(D_MODEL), _const_spec((1, D_MODEL)), _layer_spec(w),
                _const_spec((CHUNK, CHUNK)), _const_spec(gq.shape), _const_spec(gk.shape)]
    args = [x2, norm_g, w[0], bd, gq, gk]
    for tab in rope_tabs[:2 if any(permuted) else 1]:
        in_specs += [rope_spec] * 3
        args += list(tab)
    if any(permuted):
        in_specs.append(_const_spec(perm.shape))
        args.append(perm)
    out_shapes, out_specs = [], []
    for _ in permuted:
        for wd in (BRANCH, kw, kw, kw, kw):
            out_shapes.append(jax.ShapeDtypeStruct((n, wd), bf16))
            out_specs.append(row_spec(wd))
    out_shapes.append(jax.ShapeDtypeStruct((n, BRANCH), bf16))
    out_specs.append(row_spec(BRANCH))
    return pl.pallas_call(
        functools.partial(_proj_kernel, permuted=permuted, n_kv=n_kv),
        grid=(n // TILE,),
        in_specs=in_specs,
        out_specs=out_specs,
        out_shape=out_shapes,
        compiler_params=pltpu.CompilerParams(
            dimension_semantics=("parallel",), vmem_limit_bytes=VMEM_LIMIT),
        name="proj",
    )(*args)


def _slab_attention(q_tiles, k2, ks2, v2, vs2, eye, bias_t, mlo, mhi, sinks):
    npair = len(q_tiles) // 2
    lane_lo = lax.broadcasted_iota(jnp.int32, (BLOCK, LANES), 1) < HEAD_DIM
    variants = ((k2 * mlo, ks2 * mhi, v2 * mlo, vs2 * mhi),
                (ks2 * mlo, k2 * mhi, vs2 * mlo, v2 * mhi))
    results = []
    for kv in range(2):
        k_lo, k_hi, v_lo, v_hi = variants[kv]
        kx = (jnp.concatenate([k_lo, bias_t], axis=1), jnp.concatenate([k_hi, bias_t], axis=1))
        vx = (jnp.concatenate([v_lo, mlo], axis=1), jnp.concatenate([v_hi, mhi], axis=1))
        tiles = q_tiles[kv * npair:(kv + 1) * npair]
        qs = jnp.concatenate([jnp.concatenate([t, eye], axis=1) for t in tiles], axis=0)
        acc = None
        m_par, x_par = [], []
        for e in range(2):
            s_all = lax.dot_general(qs, kx[e], (((1,), (1,)), ((), ())), preferred_element_type=f32)
            ps, ms, xs = [], [], []
            for i in range(npair):
                s = s_all[BLOCK * i:BLOCK * (i + 1)]
                m = jnp.max(s, axis=-1, keepdims=True)
                if sinks is not None:
                    sk = sinks[kv * npair + i][e]
                    m = jnp.maximum(m, sk)
                    xs.append(jnp.exp2(sk - m))
                ps.append(jnp.exp2(s - m).astype(bf16))
                ms.append(m)
            ol = jnp.dot(jnp.concatenate(ps, axis=0), vx[e], preferred_element_type=f32)
            acc = ol if acc is None else acc + ol
            m_par.append(ms)
            x_par.append(xs)
        for i in range(npair):
            a = acc[BLOCK * i:BLOCK * (i + 1)]
            den = a[:, LANES:]
            if sinks is not None:
                den = den + jnp.where(lane_lo, x_par[0][i], x_par[1][i])
            o = a[:, :LANES] / den
            lse2 = jnp.where(lane_lo, m_par[0][i], m_par[1][i]) + jnp.log2(den)
            results.append((o, lse2))
    return results


def _compact_lse(lses):
    lane = lax.broadcasted_iota(jnp.int32, (BLOCK, LANES), 1) % HEAD_DIM
    out = jnp.zeros((BLOCK, LANES), f32)
    for j, l in enumerate(lses):
        out = jnp.where(lane == j, l, out)
    return out


def _attn_dense_kernel(*refs, n_kv, gated):
    (q_ref, kp_ref, kc_ref, ksp_ref, ksc_ref, vp_ref, vc_ref, vsp_ref, vsc_ref,
     eye_ref, bias_ref, mlo_ref, mhi_ref) = refs[:13]
    if gated:
        sink_ref, sg_ref, x_ref, wo_ref, xo_ref, y_buf = refs[13:]
    else:
        o_ref, lse_ref = refs[13:]
    n_slab = n_kv // 2
    tiles_per_slab = Q_TILES // n_slab
    eye, mlo, mhi = eye_ref[...], mlo_ref[...], mhi_ref[...]
    first_step = pl.program_id(1) == 0

    def keys(p_ref, c_ref, i, ls):
        if i == 0:
            return jnp.concatenate([p_ref[:, ls], c_ref[0:BLOCK, ls]], axis=0)
        return c_ref[BLOCK * (i - 1):BLOCK * (i + 1), ls]

    for i in range(ATT_ROWS // BLOCK):
        rq = slice(BLOCK * i, BLOCK * (i + 1))
        bias_t = bias_ref[jnp.where(first_step, 0, 1)] if i == 0 else bias_ref[1]
        for sl in range(n_slab):
            ls = slice(LANES * sl, LANES * (sl + 1))
            t0 = sl * tiles_per_slab
            q_tiles = [q_ref[rq, LANES * j:LANES * (j + 1)] for j in range(t0, t0 + tiles_per_slab)]
            sinks = None
            if gated:
                sinks = [(sink_ref[2 * j], sink_ref[2 * j + 1]) for j in range(t0, t0 + tiles_per_slab)]
            res = _slab_attention(q_tiles, keys(kp_ref, kc_ref, i, ls), keys(ksp_ref, ksc_ref, i, ls),
                                  keys(vp_ref, vc_ref, i, ls), keys(vsp_ref, vsc_ref, i, ls),
                                  eye, bias_t, mlo, mhi, sinks)
            for j, (o, lse2) in enumerate(res):
                cs = slice(LANES * (t0 + j), LANES * (t0 + j + 1))
                if gated:
                    y_buf[rq, cs] = (o * sg_ref[rq, cs].astype(f32)).astype(bf16)
                else:
                    o_ref[rq, cs] = o.astype(bf16)
            if not gated:
                lse_ref[sl, rq, :] = _compact_lse([l for _, l in res])
    if gated:
        y = y_buf[...]
        for c in range(0, D_MODEL, WIDE):
            xo_ref[:, c:c + WIDE] = x_ref[:, c:c + WIDE] + jnp.dot(
                y, wo_ref[:, c:c + WIDE], preferred_element_type=f32)


def _attention_dense(q, k, ksw, v, vsw, consts, *, batch, seq, n_kv, gate_args=None):
    n = batch * seq
    kw = n_kv * HEAD_DIM
    gated = gate_args is not None
    per_seq = seq // ATT_ROWS
    ratio = ATT_ROWS // BLOCK
    cur = lambda wd: pl.BlockSpec((ATT_ROWS, wd), lambda b, j: (b * per_seq + j, 0))
    prev = lambda wd: pl.BlockSpec(
        (BLOCK, wd), lambda b, j: ((b * per_seq + j) * ratio - jnp.minimum(j, 1), 0))
    eye, bias, mlo, mhi = consts
    in_specs = [cur(BRANCH), prev(kw), cur(kw), prev(kw), cur(kw), prev(kw), cur(kw), prev(kw), cur(kw),
                _const_spec(eye.shape), _const_spec(bias.shape), _const_spec(mlo.shape),
                _const_spec(mhi.shape)]
    args = [q, k, k, ksw, ksw, v, v, vsw, vsw, eye, bias, mlo, mhi]
    scratch = []
    if gated:
        sinks, sg, x2, w_out = gate_args
        in_specs += [pl.BlockSpec(memory_space=pltpu.SMEM), cur(BRANCH), cur(D_MODEL),
                     _layer_spec(w_out)]
        args += [sinks, sg, x2, w_out[0]]
        out_shape = jax.ShapeDtypeStruct((n, D_MODEL), f32)
        out_specs = cur(D_MODEL)
        scratch = scratch + [pltpu.VMEM((ATT_ROWS, BRANCH), bf16)]
    else:
        out_shape = [jax.ShapeDtypeStruct((n, BRANCH), bf16),
                     jax.ShapeDtypeStruct((n_kv // 2, n, LANES), f32)]
        out_specs = [cur(BRANCH),
                     pl.BlockSpec((n_kv // 2, ATT_ROWS, LANES), lambda b, j: (0, b * per_seq + j, 0))]
    return pl.pallas_call(
        functools.partial(_attn_dense_kernel, n_kv=n_kv, gated=gated),
        grid=(batch, per_seq),
        in_specs=in_specs,
        out_specs=out_specs,
        out_shape=out_shape,
        scratch_shapes=scratch,
        compiler_params=pltpu.CompilerParams(
            dimension_semantics=("parallel", "arbitrary"), vmem_limit_bytes=VMEM_LIMIT),
        name="attn_dense_a" if gated else "attn_dense_b",
    )(*args)


def _attn_runs_kernel(q_ref, kp_ref, kc_ref, ksp_ref, ksc_ref, vp_ref, vc_ref, vsp_ref, vsc_ref,
                      eye_ref, bias_ref, mlo_ref, mhi_ref, o_ref, lse_ref, *, dilation, n_kv):
    eye, mlo, mhi = eye_ref[...], mlo_ref[...], mhi_ref[...]
    n_slab = n_kv // 2
    tiles_per_slab = Q_TILES // n_slab
    chunk = BLOCK * dilation
    run_pitch = chunk // N_RUNS

    def residue(c, h):
        def runs_at(base):
            if isinstance(c, int):
                return [slice(base + i * run_pitch + c * RUN, base + i * run_pitch + (c + 1) * RUN)
                        for i in range(N_RUNS)]
            return [pl.ds(pl.multiple_of(base + i * run_pitch + c * RUN, RUN), RUN) for i in range(N_RUNS)]
        cur_runs = runs_at(h * chunk)
        gather = lambda ref, runs, ls: jnp.concatenate([ref[r, ls] for r in runs], axis=0)
        if h == 0:
            both = lambda p_ref, c_ref, ls: jnp.concatenate(
                [gather(p_ref, runs_at(0), ls), gather(c_ref, cur_runs, ls)], axis=0)
            bias_t = bias_ref[jnp.minimum(pl.program_id(1), 1)]
        else:
            both = lambda p_ref, c_ref, ls: jnp.concatenate(
                [gather(c_ref, runs_at((h - 1) * chunk), ls), gather(c_ref, cur_runs, ls)], axis=0)
            bias_t = bias_ref[1]
        for sl in range(n_slab):
            ls = slice(LANES * sl, LANES * (sl + 1))
            t0 = sl * tiles_per_slab
            q_tiles = [gather(q_ref, cur_runs, slice(LANES * j, LANES * (j + 1)))
                       for j in range(t0, t0 + tiles_per_slab)]
            res = _slab_attention(q_tiles, both(kp_ref, kc_ref, ls), both(ksp_ref, ksc_ref, ls),
                                  both(vp_ref, vc_ref, ls), both(vsp_ref, vsc_ref, ls),
                                  eye, bias_t, mlo, mhi, None)
            lse = _compact_lse([l for _, l in res])
            for i, r in enumerate(cur_runs):
                rr = slice(RUN * i, RUN * (i + 1))
                for j, (o, _) in enumerate(res):
                    o_ref[r, LANES * (t0 + j):LANES * (t0 + j + 1)] = o[rr].astype(bf16)
                lse_ref[sl, r, :] = lse[rr]

    n_chunks = q_ref.shape[0] // chunk
    if n_chunks * dilation <= ATT_ROWS // BLOCK:
        for h in range(n_chunks):
            for c in range(dilation):
                residue(c, h)
    else:
        for h in range(n_chunks):
            def body(c, carry, h=h):
                residue(c, h)
                return carry
            lax.fori_loop(0, dilation, body, 0, unroll=ATT_ROWS // BLOCK)


def _attention_runs(q, k, ksw, v, vsw, consts, *, batch, seq, dilation, n_kv):
    n = batch * seq
    kw = n_kv * HEAD_DIM
    chunk = BLOCK * dilation
    rows = max(chunk, ATT_ROWS)
    per_seq = seq // rows
    ratio = rows // chunk
    cur = lambda wd: pl.BlockSpec((rows, wd), lambda b, j: (b * per_seq + j, 0))
    prev = lambda wd: pl.BlockSpec(
        (chunk, wd), lambda b, j: ((b * per_seq + j) * ratio - jnp.minimum(j, 1), 0))
    eye, bias, mlo, mhi = consts
    return pl.pallas_call(
        functools.partial(_attn_runs_kernel, dilation=dilation, n_kv=n_kv),
        grid=(batch, per_seq),
        in_specs=[cur(BRANCH), prev(kw), cur(kw), prev(kw), cur(kw), prev(kw), cur(kw), prev(kw), cur(kw),
                  _const_spec(eye.shape), _const_spec(bias.shape), _const_spec(mlo.shape),
                  _const_spec(mhi.shape)],
        out_specs=[cur(BRANCH),
                   pl.BlockSpec((n_kv // 2, rows, LANES), lambda b, j: (0, b * per_seq + j, 0))],
        out_shape=[jax.ShapeDtypeStruct((n, BRANCH), bf16),
                   jax.ShapeDtypeStruct((n_kv // 2, n, LANES), f32)],
        compiler_params=pltpu.CompilerParams(
            dimension_semantics=("parallel", "arbitrary"), vmem_limit_bytes=VMEM_LIMIT),
        name=f"attn_runs_d{dilation}",
    )(q, k, k, ksw, ksw, v, v, vsw, vsw, eye, bias, mlo, mhi)


def _merge_kernel(x_ref, o0_ref, l0_ref, o1_ref, l1_ref, o2_ref, l2_ref, sg_ref, ex_ref, unperm_ref,
                  w_ref, o_ref, lnat):
    n_slab = l0_ref.shape[0]
    o12 = jnp.dot(unperm_ref[...], jnp.concatenate([o1_ref[...], o2_ref[...]], axis=1),
                  preferred_element_type=f32)
    for gi, l_ref in enumerate((l1_ref, l2_ref)):
        for s in range(n_slab):
            for c in range(MAX_DIL):
                lnat[gi * n_slab + s, pl.ds(c, RUN, stride=MAX_DIL), :] = l_ref[s, RUN * c:RUN * (c + 1), :]
    ex = ex_ref[...]
    wts = [[], []]
    for s in range(n_slab):
        l0, l1, l2 = l0_ref[s], lnat[s], lnat[n_slab + s]
        mx = jnp.maximum(jnp.maximum(l0, l1), l2)
        e = [jnp.exp2(l0 - mx), jnp.exp2(l1 - mx), jnp.exp2(l2 - mx)]
        inv = 1.0 / (e[0] + e[1] + e[2])
        for g in range(2):
            wg = e[g] * inv
            hi = wg.astype(bf16)
            lo = (wg - hi.astype(f32)).astype(bf16)
            wts[g].append(jnp.dot(jnp.concatenate([hi, lo], axis=1), ex, preferred_element_type=f32))
    w0, w1 = (jnp.concatenate(w, axis=1) for w in wts)
    w2 = 1.0 - w0 - w1
    y = ((w0 * o0_ref[...].astype(f32) + w1 * o12[:, :BRANCH] + w2 * o12[:, BRANCH:])
         * sg_ref[...].astype(f32)).astype(bf16)
    for c in range(0, D_MODEL, WIDE):
        o_ref[:, c:c + WIDE] = x_ref[:, c:c + WIDE] + jnp.dot(
            y, w_ref[:, c:c + WIDE], preferred_element_type=f32)


def _merge_project(x2, o0, l0, o1, l1, o2, l2, sg, ex, unperm, w):
    n = x2.shape[0]
    row = pl.BlockSpec((TILE, D_MODEL), lambda i: (i, 0))
    slab = pl.BlockSpec((l0.shape[0], TILE, LANES), lambda i: (0, i, 0))
    return pl.pallas_call(
        _merge_kernel,
        grid=(n // TILE,),
        in_specs=[row, row, slab, row, slab, row, slab, row, _const_spec(ex.shape),
                  _const_spec(unperm.shape), _layer_spec(w)],
        out_specs=row,
        out_shape=jax.ShapeDtypeStruct((n, D_MODEL), f32),
        scratch_shapes=[pltpu.VMEM((2 * l0.shape[0], TILE, LANES), f32)],
        input_output_aliases={0: 0},
        compiler_params=pltpu.CompilerParams(
            dimension_semantics=("parallel",), vmem_limit_bytes=VMEM_LIMIT),
        name="merge_outproj",
    )(x2, o0, l0, o1, l1, o2, l2, sg, ex, unperm, w[0])


def _rope_tables(seq):
    pos = np.arange(seq, dtype=np.float64)
    inv = ROPE_THETA ** (-np.arange(0, ROT_DIM, 2, dtype=np.float64) / ROT_DIM)
    ang = pos[:, None] * inv[None, :]
    cos, sin = np.cos(ang), np.sin(ang)
    half = ROT_DIM // 2
    pad = HEAD_DIM - ROT_DIM
    one = np.ones((seq, pad))
    zero = np.zeros((seq, pad))
    zh = np.zeros((seq, half))
    c = np.concatenate([cos, cos, one], axis=1)
    s1 = np.concatenate([-sin, zh, zero], axis=1)
    s2 = np.concatenate([zh, sin, zero], axis=1)
    tile = lambda a: np.tile(a, (1, LANES // HEAD_DIM)).astype(np.float32)
    return tile(c), tile(s1), tile(s2)


def _bias_tables(max_dist, order):
    qi = order[None, :]
    kj = np.concatenate([order, order + BLOCK])[:, None]
    dist = BLOCK + qi - kj
    ok = (dist >= 0) & (dist <= max_dist)
    first = ok & (kj >= BLOCK)
    return np.stack([np.where(first, 0.0, NEG), np.where(ok, 0.0, NEG)]).astype(bf16)


def kernel(x, norm_a, w_in_a, q_gain_a, k_gain_a, sinks_a, w_out_a,
           norm_b, w_in_b, q_gain_b, k_gain_b, w_out_b):
    batch, seq, _ = x.shape
    n = batch * seq
    hid = np.arange(CHUNK) // HEAD_DIM
    bd = np.where(hid[:, None] == hid[None, :], 1.0 / HEAD_DIM, 0.0).astype(bf16)
    lane = np.arange(LANES)[None, :]
    mlo = np.broadcast_to(lane < HEAD_DIM, (2 * BLOCK, LANES)).astype(bf16)
    mhi = np.broadcast_to(lane >= HEAD_DIM, (2 * BLOCK, LANES)).astype(bf16)
    eye = np.eye(BLOCK).astype(bf16)
    natural = np.arange(BLOCK)
    consts_a = (eye, _bias_tables(A_WINDOW - 1, natural), mlo, mhi)

    r = np.arange(TILE)
    dest = RUN * (r % MAX_DIL) + r // MAX_DIL
    perm = (np.arange(TILE)[:, None] == dest[None, :]).astype(bf16)
    unperm = np.ascontiguousarray(perm.T)
    src = MAX_DIL * (r % RUN) + r // RUN
    rope_nat = _rope_tables(seq)
    rope_perm = tuple(t.reshape(seq // TILE, TILE, LANES)[:, src, :].reshape(seq, LANES) for t in rope_nat)
    order = {}
    for _, d in B_PAIRS:
        if d == 1 or d == MAX_DIL:
            order[d] = natural
        else:
            assert BLOCK * d == TILE and MAX_DIL % d == 0
            step = MAX_DIL // d
            order[d] = step * (natural % RUN) + natural // RUN
    consts_b = {d: (eye, _bias_tables(BLOCK, order[d]), mlo, mhi) for _, d in B_PAIRS}

    half = BRANCH // (B_KV // 2)
    col_tile = np.arange(half)[None, :] // LANES
    col_odd = (np.arange(half)[None, :] % LANES) // HEAD_DIM
    row = np.arange(LANES)[:, None]
    ex = ((row % HEAD_DIM == col_tile) & (row // HEAD_DIM == col_odd)).astype(bf16)
    ex = np.concatenate([ex, ex], axis=0)
    tile_gain = lambda g: jnp.tile(g, (1, CHUNK // HEAD_DIM))[:, None, :]
    q_scale = SCALE * LOG2E
    w_in_a, w_out_a, w_in_b, w_out_b = (w.astype(bf16) for w in (w_in_a, w_out_a, w_in_b, w_out_b))

    x2 = x.reshape(n, D_MODEL)
    depth = norm_a.shape[0] + norm_b.shape[0]
    for layer in range(depth):
        idx = layer // 2
        if layer % 2 == 0:
            q, k, ksw, v, vsw, sg = _project(
                x2, norm_a[idx][None], (w_in_a, idx), bd,
                tile_gain(q_gain_a[idx][None] * q_scale), tile_gain(k_gain_a[idx][None]),
                (rope_nat,), None, permuted=(False,), n_kv=A_KV, seq=seq)
            x2 = _attention_dense(
                q, k, ksw, v, vsw, consts_a, batch=batch, seq=seq, n_kv=A_KV,
                gate_args=(sinks_a[idx].astype(f32) * LOG2E, sg, x2, (w_out_a, idx)))
        else:
            permuted = tuple(d > 1 for _, d in B_PAIRS)
            outs = _project(
                x2, norm_b[idx][None], (w_in_b, idx), bd,
                tile_gain(q_gain_b[idx] * q_scale), tile_gain(k_gain_b[idx]),
                (rope_nat, rope_perm), perm, permuted=permuted, n_kv=B_KV, seq=seq)
            sg = outs[-1]
            merged = []
            for g, (window, dilation) in enumerate(B_PAIRS):
                assert window // dilation == BLOCK
                qkv = outs[5 * g:5 * g + 5]
                if dilation == 1:
                    merged += _attention_dense(*qkv, consts_b[1], batch=batch, seq=seq, n_kv=B_KV)
                else:
                    merged += _attention_runs(*qkv, consts_b[dilation], batch=batch, seq=seq,
                                              dilation=dilation, n_kv=B_KV)
            x2 = _merge_project(x2, *merged, sg, ex, unperm, (w_out_b, idx))
    return x2.reshape(batch, seq, D_MODEL)
```

```python
import functools
import math

import jax
import jax.numpy as jnp
import numpy as np
from jax import lax
from jax.experimental import pallas as pl
from jax.experimental.pallas import tpu as pltpu

D_MODEL = 1024
HEAD_DIM = 64
N_HEADS = 16
BRANCH = N_HEADS * HEAD_DIM
A_KV = 2
B_KV = 4
B_PAIRS = ((128, 1), (512, 4), (2048, 16))
A_WINDOW = 128
BLOCK = 128
ROT_DIM = HEAD_DIM // 4
ROPE_THETA = 500000.0
EPS = 1e-6
SCALE = HEAD_DIM ** -0.5
LOG2E = math.log2(math.e)
LANES = 128
Q_TILES = BRANCH // LANES
NEG = -1e30

TILE = 512
ATT_ROWS = 1024
CHUNK = 256
WIDE = 512
MAX_DIL = 16
RUN = TILE // MAX_DIL
N_RUNS = BLOCK // RUN
VMEM_LIMIT = 56 * 1024 * 1024

bf16 = jnp.bfloat16
f32 = jnp.float32


def _const_spec(shape):
    nd = len(shape)
    return pl.BlockSpec(shape, lambda *_: (0,) * nd, pipeline_mode=pl.Buffered(1))


def _layer_spec(stacked_and_index):
    stacked, index = stacked_and_index
    return pl.BlockSpec((None,) + stacked.shape[1:], lambda *_: (index, 0, 0), pipeline_mode=pl.Buffered(1))


def _lane_tiles(z):
    return [z[:, i:i + LANES] for i in range(0, z.shape[1], LANES)]


def _proj_kernel(*refs, permuted, n_kv):
    x_ref, ng_ref, w_ref, bd_ref, gq_ref, gk_ref = refs[:6]
    n_tab = 2 if any(permuted) else 1
    tabs = [tuple(r[...] for r in refs[6 + 3 * i:9 + 3 * i]) for i in range(n_tab)]
    pos = 6 + 3 * n_tab
    if any(permuted):
        perm_ref = refs[pos]
        pos += 1
    out_refs = refs[pos:]

    x = x_ref[...]
    ms = jnp.mean(x * x, axis=-1, keepdims=True)
    h = ((x * lax.rsqrt(ms + EPS)) * ng_ref[...]).astype(bf16)
    hs = [h]
    if any(permuted):
        hs.append(jnp.dot(perm_ref[...], h, preferred_element_type=f32).astype(bf16))
    bd = bd_ref[...]

    def roll_tiles(t, shift):
        return jnp.concatenate([pltpu.roll(a, shift, axis=1) for a in _lane_tiles(t)], axis=1)

    def head_norm_rope(z, gain, tab):
        w = z.shape[1]
        cos, sn1, sn2 = (jnp.concatenate([a] * (w // LANES), axis=1) for a in tab)
        msq = jnp.dot((z * z).astype(bf16), bd[:w, :w], preferred_element_type=f32)
        t = (z * lax.rsqrt(msq + EPS)) * gain[:, :w]
        up = roll_tiles(t, LANES - ROT_DIM // 2)
        dn = roll_tiles(t, ROT_DIM // 2)
        return t * cos + up * sn1 + dn * sn2

    kw = n_kv * HEAD_DIM
    col = 0
    oi = 0
    for g, perm in enumerate(permuted):
        hg, tab = hs[int(perm)], tabs[int(perm)]
        q_ref, k_ref, ksw_ref, v_ref, vsw_ref = out_refs[oi:oi + 5]
        oi += 5
        for c in range(0, BRANCH, WIDE):
            zz = jnp.dot(hg, w_ref[:, col + c:col + c + WIDE], preferred_element_type=f32)
            for cc in range(0, WIDE, CHUNK):
                q_ref[:, c + cc:c + cc + CHUNK] = head_norm_rope(
                    zz[:, cc:cc + CHUNK], gq_ref[g], tab).astype(bf16)
        col += BRANCH
        zz = jnp.dot(hg, w_ref[:, col:col + 2 * kw], preferred_element_type=f32)
        k = head_norm_rope(zz[:, :kw], gk_ref[g], tab)
        k_ref[...] = k.astype(bf16)
        ksw_ref[...] = roll_tiles(k, LANES // 2).astype(bf16)
        v = zz[:, kw:]
        v_ref[...] = v.astype(bf16)
        vsw_ref[...] = roll_tiles(v, LANES // 2).astype(bf16)
        col += 2 * kw
    sg_ref = out_refs[oi]
    for c in range(0, BRANCH, WIDE):
        gt = jnp.dot(h, w_ref[:, col + c:col + c + WIDE], preferred_element_type=f32)
        sg_ref[:, c:c + WIDE] = (gt * (1.0 / (1.0 + jnp.exp(-gt)))).astype(bf16)


def _project(x2, norm_g, w, bd, gq, gk, rope_tabs, perm, *, permuted, n_kv, seq):
    n = x2.shape[0]
    kw = n_kv * HEAD_DIM
    rows = TILE if any(permuted) else 2 * TILE
    per_seq = seq // rows
    row_spec = lambda wd: pl.BlockSpec((rows, wd), lambda i: (i, 0))
    rope_spec = pl.BlockSpec((rows, LANES), lambda i: (i % per_seq, 0))
    in_specs = [row_spec(D_MODEL), _const_spec((1, D_MODEL)), _layer_spec(w),
                _const_spec((CHUNK, CHUNK)), _const_spec(gq.shape), _const_spec(gk.shape)]
    args = [x2, norm_g, w[0], bd, gq, gk]
    for tab in rope_tabs[:2 if any(permuted) else 1]:
        in_specs += [rope_spec] * 3
        args += list(tab)
    if any(permuted):
        in_specs.append(_const_spec(perm.shape))
        args.append(perm)
    out_shapes, out_specs = [], []
    for _ in permuted:
        for wd in (BRANCH, kw, kw, kw, kw):
            out_shapes.append(jax.ShapeDtypeStruct((n, wd), bf16))
            out_specs.append(row_spec(wd))
    out_shapes.append(jax.ShapeDtypeStruct((n, BRANCH), bf16))
    out_specs.append(row_spec(BRANCH))
    return pl.pallas_call(
        functools.partial(_proj_kernel, permuted=permuted, n_kv=n_kv),
        grid=(n // rows,),
        in_specs=in_specs,
        out_specs=out_specs,
        out_shape=out_shapes,
        compiler_params=pltpu.CompilerParams(
            dimension_semantics=("parallel",), vmem_limit_bytes=VMEM_LIMIT),
        name="proj",
    )(*args)


def _slab_attention(q_tiles, k2, ks2, v2, vs2, eye, bias_t, mlo, mhi, sinks):
    npair = len(q_tiles) // 2
    lane_lo = lax.broadcasted_iota(jnp.int32, (BLOCK, LANES), 1) < HEAD_DIM
    variants = ((k2 * mlo, ks2 * mhi, v2 * mlo, vs2 * mhi),
                (ks2 * mlo, k2 * mhi, vs2 * mlo, v2 * mhi))
    results = []
    for kv in range(2):
        k_lo, k_hi, v_lo, v_hi = variants[kv]
        kx = (jnp.concatenate([k_lo, bias_t], axis=1), jnp.concatenate([k_hi, bias_t], axis=1))
        vx = (jnp.concatenate([v_lo, mlo], axis=1), jnp.concatenate([v_hi, mhi], axis=1))
        tiles = q_tiles[kv * npair:(kv + 1) * npair]
        qs = jnp.concatenate([jnp.concatenate([t, eye], axis=1) for t in tiles], axis=0)
        acc = None
        m_par, x_par = [], []
        for e in range(2):
            s_all = lax.dot_general(qs, kx[e], (((1,), (1,)), ((), ())), preferred_element_type=f32)
            ps, ms, xs = [], [], []
            for i in range(npair):
                s = s_all[BLOCK * i:BLOCK * (i + 1)].astype(bf16)
                mb = jnp.max(s, axis=-1, keepdims=True)
                if sinks is not None:
                    sk = sinks[kv * npair + i][e]
                    mb = jnp.maximum(mb.astype(f32), sk).astype(bf16)
                m = mb.astype(f32)
                if sinks is not None:
                    xs.append(jnp.exp2(sk - m))
                ps.append(jnp.exp2(s - mb))
                ms.append(m)
            ol = jnp.dot(jnp.concatenate(ps, axis=0), vx[e], preferred_element_type=f32)
            acc = ol if acc is None else acc + ol
            m_par.append(ms)
            x_par.append(xs)
        for i in range(npair):
            a = acc[BLOCK * i:BLOCK * (i + 1)]
            den = a[:, LANES:]
            if sinks is not None:
                den = den + jnp.where(lane_lo, x_par[0][i], x_par[1][i])
            o = a[:, :LANES] / den
            lse2 = jnp.where(lane_lo, m_par[0][i], m_par[1][i]) + jnp.log2(den)
            results.append((o, lse2))
    return results


def _compact_lse(lses):
    lane = lax.broadcasted_iota(jnp.int32, (BLOCK, LANES), 1) % HEAD_DIM
    out = jnp.zeros((BLOCK, LANES), f32)
    for j, l in enumerate(lses):
        out = jnp.where(lane == j, l, out)
    return out


def _attn_dense_kernel(*refs, n_kv, gated):
    (q_ref, kp_ref, kc_ref, ksp_ref, ksc_ref, vp_ref, vc_ref, vsp_ref, vsc_ref,
     eye_ref, bias_ref, mlo_ref, mhi_ref) = refs[:13]
    if gated:
        sink_ref, sg_ref, x_ref, wo_ref, xo_ref, y_buf = refs[13:]
    else:
        o_ref, lse_ref = refs[13:]
    n_slab = n_kv // 2
    tiles_per_slab = Q_TILES // n_slab
    eye, mlo, mhi = eye_ref[...], mlo_ref[...], mhi_ref[...]
    first_step = pl.program_id(1) == 0

    def keys(p_ref, c_ref, i, ls):
        if i == 0:
            return jnp.concatenate([p_ref[:, ls], c_ref[0:BLOCK, ls]], axis=0)
        return c_ref[BLOCK * (i - 1):BLOCK * (i + 1), ls]

    for i in range(ATT_ROWS // BLOCK):
        rq = slice(BLOCK * i, BLOCK * (i + 1))
        bias_t = bias_ref[jnp.where(first_step, 0, 1)] if i == 0 else bias_ref[1]
        for sl in range(n_slab):
            ls = slice(LANES * sl, LANES * (sl + 1))
            t0 = sl * tiles_per_slab
            q_tiles = [q_ref[rq, LANES * j:LANES * (j + 1)] for j in range(t0, t0 + tiles_per_slab)]
            sinks = None
            if gated:
                sinks = [(sink_ref[2 * j], sink_ref[2 * j + 1]) for j in range(t0, t0 + tiles_per_slab)]
            res = _slab_attention(q_tiles, keys(kp_ref, kc_ref, i, ls), keys(ksp_ref, ksc_ref, i, ls),
                                  keys(vp_ref, vc_ref, i, ls), keys(vsp_ref, vsc_ref, i, ls),
                                  eye, bias_t, mlo, mhi, sinks)
            for j, (o, lse2) in enumerate(res):
                cs = slice(LANES * (t0 + j), LANES * (t0 + j + 1))
                if gated:
                    y_buf[rq, cs] = (o * sg_ref[rq, cs].astype(f32)).astype(bf16)
                else:
                    o_ref[rq, cs] = o.astype(bf16)
            if not gated:
                lse_ref[sl, rq, :] = _compact_lse([l for _, l in res])
    if gated:
        y = y_buf[...]
        for c in range(0, D_MODEL, WIDE):
            xo_ref[:, c:c + WIDE] = x_ref[:, c:c + WIDE] + jnp.dot(
                y, wo_ref[:, c:c + WIDE], preferred_element_type=f32)


def _attention_dense(q, k, ksw, v, vsw, consts, *, batch, seq, n_kv, gate_args=None):
    n = batch * seq
    kw = n_kv * HEAD_DIM
    gated = gate_args is not None
    per_seq = seq // ATT_ROWS
    ratio = ATT_ROWS // BLOCK
    cur = lambda wd: pl.BlockSpec((ATT_ROWS, wd), lambda b, j: (b * per_seq + j, 0))
    prev = lambda wd: pl.BlockSpec(
        (BLOCK, wd), lambda b, j: ((b * per_seq + j) * ratio - jnp.minimum(j, 1), 0))
    eye, bias, mlo, mhi = consts
    in_specs = [cur(BRANCH), prev(kw), cur(kw), prev(kw), cur(kw), prev(kw), cur(kw), prev(kw), cur(kw),
                _const_spec(eye.shape), _const_spec(bias.shape), _const_spec(mlo.shape),
                _const_spec(mhi.shape)]
    args = [q, k, k, ksw, ksw, v, v, vsw, vsw, eye, bias, mlo, mhi]
    scratch = []
    if gated:
        sinks, sg, x2, w_out = gate_args
        in_specs += [pl.BlockSpec(memory_space=pltpu.SMEM), cur(BRANCH), cur(D_MODEL),
                     _layer_spec(w_out)]
        args += [sinks, sg, x2, w_out[0]]
        out_shape = jax.ShapeDtypeStruct((n, D_MODEL), f32)
        out_specs = cur(D_MODEL)
        scratch = scratch + [pltpu.VMEM((ATT_ROWS, BRANCH), bf16)]
    else:
        out_shape = [jax.ShapeDtypeStruct((n, BRANCH), bf16),
                     jax.ShapeDtypeStruct((n_kv // 2, n, LANES), f32)]
        out_specs = [cur(BRANCH),
                     pl.BlockSpec((n_kv // 2, ATT_ROWS, LANES), lambda b, j: (0, b * per_seq + j, 0))]
    return pl.pallas_call(
        functools.partial(_attn_dense_kernel, n_kv=n_kv, gated=gated),
        grid=(batch, per_seq),
        in_specs=in_specs,
        out_specs=out_specs,
        out_shape=out_shape,
        scratch_shapes=scratch,
        compiler_params=pltpu.CompilerParams(
            dimension_semantics=("parallel", "arbitrary"), vmem_limit_bytes=VMEM_LIMIT),
        name="attn_dense_a" if gated else "attn_dense_b",
    )(*args)


def _attn_runs_kernel(q_ref, kp_ref, kc_ref, ksp_ref, ksc_ref, vp_ref, vc_ref, vsp_ref, vsc_ref,
                      eye_ref, bias_ref, mlo_ref, mhi_ref, o_ref, lse_ref, *, dilation, n_kv):
    eye, mlo, mhi = eye_ref[...], mlo_ref[...], mhi_ref[...]
    n_slab = n_kv // 2
    tiles_per_slab = Q_TILES // n_slab
    chunk = BLOCK * dilation
    run_pitch = chunk // N_RUNS

    def residue(c, h):
        def runs_at(base):
            if isinstance(c, int):
                return [slice(base + i * run_pitch + c * RUN, base + i * run_pitch + (c + 1) * RUN)
                        for i in range(N_RUNS)]
            return [pl.ds(pl.multiple_of(base + i * run_pitch + c * RUN, RUN), RUN) for i in range(N_RUNS)]
        cur_runs = runs_at(h * chunk)
        gather = lambda ref, runs, ls: jnp.concatenate([ref[r, ls] for r in runs], axis=0)
        if h == 0:
            both = lambda p_ref, c_ref, ls: jnp.concatenate(
                [gather(p_ref, runs_at(0), ls), gather(c_ref, cur_runs, ls)], axis=0)
            bias_t = bias_ref[jnp.minimum(pl.program_id(1), 1)]
        else:
            both = lambda p_ref, c_ref, ls: jnp.concatenate(
                [gather(c_ref, runs_at((h - 1) * chunk), ls), gather(c_ref, cur_runs, ls)], axis=0)
            bias_t = bias_ref[1]
        for sl in range(n_slab):
            ls = slice(LANES * sl, LANES * (sl + 1))
            t0 = sl * tiles_per_slab
            q_tiles = [gather(q_ref, cur_runs, slice(LANES * j, LANES * (j + 1)))
                       for j in range(t0, t0 + tiles_per_slab)]
            res = _slab_attention(q_tiles, both(kp_ref, kc_ref, ls), both(ksp_ref, ksc_ref, ls),
                                  both(vp_ref, vc_ref, ls), both(vsp_ref, vsc_ref, ls),
                                  eye, bias_t, mlo, mhi, None)
            lse = _compact_lse([l for _, l in res])
            for i, r in enumerate(cur_runs):
                rr = slice(RUN * i, RUN * (i + 1))
                for j, (o, _) in enumerate(res):
                    o_ref[r, LANES * (t0 + j):LANES * (t0 + j + 1)] = o[rr].astype(bf16)
                lse_ref[sl, r, :] = lse[rr]

    n_chunks = q_ref.shape[0] // chunk
    if n_chunks * dilation <= ATT_ROWS // BLOCK:
        for h in range(n_chunks):
            for c in range(dilation):
                residue(c, h)
    else:
        for h in range(n_chunks):
            def body(c, carry, h=h):
                residue(c, h)
                return carry
            lax.fori_loop(0, dilation, body, 0, unroll=ATT_ROWS // BLOCK)


def _attention_runs(q, k, ksw, v, vsw, consts, *, batch, seq, dilation, n_kv):
    n = batch * seq
    kw = n_kv * HEAD_DIM
    chunk = BLOCK * dilation
    rows = max(chunk, ATT_ROWS)
    per_seq = seq // rows
    ratio = rows // chunk
    cur = lambda wd: pl.BlockSpec((rows, wd), lambda b, j: (b * per_seq + j, 0))
    prev = lambda wd: pl.BlockSpec(
        (chunk, wd), lambda b, j: ((b * per_seq + j) * ratio - jnp.minimum(j, 1), 0))
    eye, bias, mlo, mhi = consts
    return pl.pallas_call(
        functools.partial(_attn_runs_kernel, dilation=dilation, n_kv=n_kv),
        grid=(batch, per_seq),
        in_specs=[cur(BRANCH), prev(kw), cur(kw), prev(kw), cur(kw), prev(kw), cur(kw), prev(kw), cur(kw),
                  _const_spec(eye.shape), _const_spec(bias.shape), _const_spec(mlo.shape),
                  _const_spec(mhi.shape)],
        out_specs=[cur(BRANCH),
                   pl.BlockSpec((n_kv // 2, rows, LANES), lambda b, j: (0, b * per_seq + j, 0))],
        out_shape=[jax.ShapeDtypeStruct((n, BRANCH), bf16),
                   jax.ShapeDtypeStruct((n_kv // 2, n, LANES), f32)],
        compiler_params=pltpu.CompilerParams(
            dimension_semantics=("parallel", "arbitrary"), vmem_limit_bytes=VMEM_LIMIT),
        name=f"attn_runs_d{dilation}",
    )(q, k, k, ksw, ksw, v, v, vsw, vsw, eye, bias, mlo, mhi)


def _merge_kernel(x_ref, o0_ref, l0_ref, o1_ref, l1_ref, o2_ref, l2_ref, sg_ref, ex_ref, unperm_ref,
                  w_ref, o_ref, lnat):
    n_slab = l0_ref.shape[0]
    o12 = jnp.dot(unperm_ref[...], jnp.concatenate([o1_ref[...], o2_ref[...]], axis=1),
                  preferred_element_type=f32)
    for gi, l_ref in enumerate((l1_ref, l2_ref)):
        for s in range(n_slab):
            for c in range(MAX_DIL):
                lnat[gi * n_slab + s, pl.ds(c, RUN, stride=MAX_DIL), :] = l_ref[s, RUN * c:RUN * (c + 1), :]
    ex = ex_ref[...]
    wts = [[], []]
    for s in range(n_slab):
        l0, l1, l2 = l0_ref[s], lnat[s], lnat[n_slab + s]
        mx = jnp.maximum(jnp.maximum(l0, l1), l2)
        e = [jnp.exp2(l0 - mx), jnp.exp2(l1 - mx), jnp.exp2(l2 - mx)]
        inv = 1.0 / (e[0] + e[1] + e[2])
        for g in range(2):
            wg = e[g] * inv
            hi = wg.astype(bf16)
            lo = (wg - hi.astype(f32)).astype(bf16)
            wts[g].append(jnp.dot(jnp.concatenate([hi, lo], axis=1), ex, preferred_element_type=f32))
    w0, w1 = (jnp.concatenate(w, axis=1) for w in wts)
    w2 = 1.0 - w0 - w1
    y = ((w0 * o0_ref[...].astype(f32) + w1 * o12[:, :BRANCH] + w2 * o12[:, BRANCH:])
         * sg_ref[...].astype(f32)).astype(bf16)
    for c in range(0, D_MODEL, WIDE):
        o_ref[:, c:c + WIDE] = x_ref[:, c:c + WIDE] + jnp.dot(
            y, w_ref[:, c:c + WIDE], preferred_element_type=f32)


def _merge_project(x2, o0, l0, o1, l1, o2, l2, sg, ex, unperm, w):
    n = x2.shape[0]
    row = pl.BlockSpec((TILE, D_MODEL), lambda i: (i, 0))
    slab = pl.BlockSpec((l0.shape[0], TILE, LANES), lambda i: (0, i, 0))
    return pl.pallas_call(
        _merge_kernel,
        grid=(n // TILE,),
        in_specs=[row, row, slab, row, slab, row, slab, row, _const_spec(ex.shape),
                  _const_spec(unperm.shape), _layer_spec(w)],
        out_specs=row,
        out_shape=jax.ShapeDtypeStruct((n, D_MODEL), f32),
        scratch_shapes=[pltpu.VMEM((2 * l0.shape[0], TILE, LANES), f32)],
        input_output_aliases={0: 0},
        compiler_params=pltpu.CompilerParams(
            dimension_semantics=("parallel",), vmem_limit_bytes=VMEM_LIMIT),
        name="merge_outproj",
    )(x2, o0, l0, o1, l1, o2, l2, sg, ex, unperm, w[0])


def _rope_tables(seq):
    pos = np.arange(seq, dtype=np.float64)
    inv = ROPE_THETA ** (-np.arange(0, ROT_DIM, 2, dtype=np.float64) / ROT_DIM)
    ang = pos[:, None] * inv[None, :]
    cos, sin = np.cos(ang), np.sin(ang)
    half = ROT_DIM // 2
    pad = HEAD_DIM - ROT_DIM
    one = np.ones((seq, pad))
    zero = np.zeros((seq, pad))
    zh = np.zeros((seq, half))
    c = np.concatenate([cos, cos, one], axis=1)
    s1 = np.concatenate([-sin, zh, zero], axis=1)
    s2 = np.concatenate([zh, sin, zero], axis=1)
    tile = lambda a: np.tile(a, (1, LANES // HEAD_DIM)).astype(np.float32)
    return tile(c), tile(s1), tile(s2)


def _bias_tables(max_dist, order):
    qi = order[None, :]
    kj = np.concatenate([order, order + BLOCK])[:, None]
    dist = BLOCK + qi - kj
    ok = (dist >= 0) & (dist <= max_dist)
    first = ok & (kj >= BLOCK)
    return np.stack([np.where(first, 0.0, NEG), np.where(ok, 0.0, NEG)]).astype(bf16)


def kernel(x, norm_a, w_in_a, q_gain_a, k_gain_a, sinks_a, w_out_a,
           norm_b, w_in_b, q_gain_b, k_gain_b, w_out_b):
    batch, seq, _ = x.shape
    n = batch * seq
    hid = np.arange(CHUNK) // HEAD_DIM
    bd = np.where(hid[:, None] == hid[None, :], 1.0 / HEAD_DIM, 0.0).astype(bf16)
    lane = np.arange(LANES)[None, :]
    mlo = np.broadcast_to(lane < HEAD_DIM, (2 * BLOCK, LANES)).astype(bf16)
    mhi = np.broadcast_to(lane >= HEAD_DIM, (2 * BLOCK, LANES)).astype(bf16)
    eye = np.eye(BLOCK).astype(bf16)
    natural = np.arange(BLOCK)
    consts_a = (eye, _bias_tables(A_WINDOW - 1, natural), mlo, mhi)

    r = np.arange(TILE)
    dest = RUN * (r % MAX_DIL) + r // MAX_DIL
    perm = (np.arange(TILE)[:, None] == dest[None, :]).astype(bf16)
    unperm = np.ascontiguousarray(perm.T)
    src = MAX_DIL * (r % RUN) + r // RUN
    rope_nat = _rope_tables(seq)
    rope_perm = tuple(t.reshape(seq // TILE, TILE, LANES)[:, src, :].reshape(seq, LANES) for t in rope_nat)
    order = {}
    for _, d in B_PAIRS:
        if d == 1 or d == MAX_DIL:
            order[d] = natural
        else:
            assert BLOCK * d == TILE and MAX_DIL % d == 0
            step = MAX_DIL // d
            order[d] = step * (natural % RUN) + natural // RUN
    consts_b = {d: (eye, _bias_tables(BLOCK, order[d]), mlo, mhi) for _, d in B_PAIRS}

    half = BRANCH // (B_KV // 2)
    col_tile = np.arange(half)[None, :] // LANES
    col_odd = (np.arange(half)[None, :] % LANES) // HEAD_DIM
    row = np.arange(LANES)[:, None]
    ex = ((row % HEAD_DIM == col_tile) & (row // HEAD_DIM == col_odd)).astype(bf16)
    ex = np.concatenate([ex, ex], axis=0)
    tile_gain = lambda g: jnp.tile(g, (1, CHUNK // HEAD_DIM))[:, None, :]
    q_scale = SCALE * LOG2E
    w_in_a, w_out_a, w_in_b, w_out_b = (w.astype(bf16) for w in (w_in_a, w_out_a, w_in_b, w_out_b))

    x2 = x.reshape(n, D_MODEL)
    depth = norm_a.shape[0] + norm_b.shape[0]
    for layer in range(depth):
        idx = layer // 2
        if layer % 2 == 0:
            q, k, ksw, v, vsw, sg = _project(
                x2, norm_a[idx][None], (w_in_a, idx), bd,
                tile_gain(q_gain_a[idx][None] * q_scale), tile_gain(k_gain_a[idx][None]),
                (rope_nat,), None, permuted=(False,), n_kv=A_KV, seq=seq)
            x2 = _attention_dense(
                q, k, ksw, v, vsw, consts_a, batch=batch, seq=seq, n_kv=A_KV,
                gate_args=(sinks_a[idx].astype(f32) * LOG2E, sg, x2, (w_out_a, idx)))
        else:
            permuted = tuple(d > 1 for _, d in B_PAIRS)
            outs = _project(
                x2, norm_b[idx][None], (w_in_b, idx), bd,
                tile_gain(q_gain_b[idx] * q_scale), tile_gain(k_gain_b[idx]),
                (rope_nat, rope_perm), perm, permuted=permuted, n_kv=B_KV, seq=seq)
            sg = outs[-1]
            merged = []
            for g, (window, dilation) in enumerate(B_PAIRS):
                assert window // dilation == BLOCK
                qkv = outs[5 * g:5 * g + 5]
                if dilation == 1:
                    merged += _attention_dense(*qkv, consts_b[1], batch=batch, seq=seq, n_kv=B_KV)
                else:
                    merged += _attention_runs(*qkv, consts_b[dilation], batch=batch, seq=seq,
                                              dilation=dilation, n_kv=B_KV)
            x2 = _merge_project(x2, *merged, sg, ex, unperm, (w_out_b, idx))
    return x2.reshape(batch, seq, D_MODEL)
```

```python
import functools
import math

import jax
import jax.numpy as jnp
import numpy as np
from jax import lax
from jax.experimental import pallas as pl
from jax.experimental.pallas import tpu as pltpu

D_MODEL = 1024
HEAD_DIM = 64
N_HEADS = 16
BRANCH = N_HEADS * HEAD_DIM
A_KV = 2
B_KV = 4
B_PAIRS = ((128, 1), (512, 4), (2048, 16))
A_WINDOW = 128
BLOCK = 128
ROT_DIM = HEAD_DIM // 4
ROPE_THETA = 500000.0
EPS = 1e-6
SCALE = HEAD_DIM ** -0.5
LOG2E = math.log2(math.e)
LANES = 128
Q_TILES = BRANCH // LANES
NEG = -1e30

TILE = 512
ATT_ROWS = 1024
MERGE_TILES = 2
CHUNK = 256
WIDE = 512
MAX_DIL = 16
RUN = TILE // MAX_DIL
N_RUNS = BLOCK // RUN
VMEM_LIMIT = 56 * 1024 * 1024

bf16 = jnp.bfloat16
f32 = jnp.float32


def _const_spec(shape):
    nd = len(shape)
    return pl.BlockSpec(shape, lambda *_: (0,) * nd, pipeline_mode=pl.Buffered(1))


def _layer_spec(stacked_and_index):
    stacked, index = stacked_and_index
    return pl.BlockSpec((None,) + stacked.shape[1:], lambda *_: (index, 0, 0), pipeline_mode=pl.Buffered(1))


def _lane_tiles(z):
    return [z[:, i:i + LANES] for i in range(0, z.shape[1], LANES)]


def _proj_kernel(*refs, permuted, n_kv):
    x_ref, ng_ref, w_ref, bd_ref, gq_ref, gk_ref = refs[:6]
    n_tab = 2 if any(permuted) else 1
    tabs = [tuple(r[...] for r in refs[6 + 3 * i:9 + 3 * i]) for i in range(n_tab)]
    pos = 6 + 3 * n_tab
    if any(permuted):
        perm_ref = refs[pos]
        pos += 1
    out_refs = refs[pos:]

    x = x_ref[...]
    ms = jnp.mean(x * x, axis=-1, keepdims=True)
    h = ((x * lax.rsqrt(ms + EPS)) * ng_ref[...]).astype(bf16)
    hs = [h]
    if any(permuted):
        hs.append(jnp.dot(perm_ref[...], h, preferred_element_type=f32).astype(bf16))
    bd = bd_ref[...]

    def roll_tiles(t, shift):
        return jnp.concatenate([pltpu.roll(a, shift, axis=1) for a in _lane_tiles(t)], axis=1)

    def head_norm_rope(z, gain, tab):
        w = z.shape[1]
        cos, sn1, sn2 = (jnp.concatenate([a] * (w // LANES), axis=1) for a in tab)
        msq = jnp.dot((z * z).astype(bf16), bd[:w, :w], preferred_element_type=f32)
        t = (z * lax.rsqrt(msq + EPS)) * gain[:, :w]
        up = roll_tiles(t, LANES - ROT_DIM // 2)
        dn = roll_tiles(t, ROT_DIM // 2)
        return t * cos + up * sn1 + dn * sn2

    kw = n_kv * HEAD_DIM
    col = 0
    oi = 0
    for g, perm in enumerate(permuted):
        hg, tab = hs[int(perm)], tabs[int(perm)]
        q_ref, k_ref, ksw_ref, v_ref, vsw_ref = out_refs[oi:oi + 5]
        oi += 5
        for c in range(0, BRANCH, WIDE):
            zz = jnp.dot(hg, w_ref[:, col + c:col + c + WIDE], preferred_element_type=f32)
            for cc in range(0, WIDE, CHUNK):
                q_ref[:, c + cc:c + cc + CHUNK] = head_norm_rope(
                    zz[:, cc:cc + CHUNK], gq_ref[g], tab).astype(bf16)
        col += BRANCH
        zz = jnp.dot(hg, w_ref[:, col:col + 2 * kw], preferred_element_type=f32)
        k = head_norm_rope(zz[:, :kw], gk_ref[g], tab)
        k_ref[...] = k.astype(bf16)
        ksw_ref[...] = roll_tiles(k, LANES // 2).astype(bf16)
        v = zz[:, kw:]
        v_ref[...] = v.astype(bf16)
        vsw_ref[...] = roll_tiles(v, LANES // 2).astype(bf16)
        col += 2 * kw
    sg_ref = out_refs[oi]
    for c in range(0, BRANCH, WIDE):
        gt = jnp.dot(h, w_ref[:, col + c:col + c + WIDE], preferred_element_type=f32)
        sg_ref[:, c:c + WIDE] = (gt * (1.0 / (1.0 + jnp.exp(-gt)))).astype(bf16)


def _project(x2, norm_g, w, bd, gq, gk, rope_tabs, perm, *, permuted, n_kv, seq):
    n = x2.shape[0]
    kw = n_kv * HEAD_DIM
    rows = TILE if any(permuted) else 2 * TILE
    per_seq = seq // rows
    row_spec = lambda wd: pl.BlockSpec((rows, wd), lambda i: (i, 0))
    rope_spec = pl.BlockSpec((rows, LANES), lambda i: (i % per_seq, 0))
    in_specs = [row_spec(D_MODEL), _const_spec((1, D_MODEL)), _layer_spec(w),
                _const_spec((CHUNK, CHUNK)), _const_spec(gq.shape), _const_spec(gk.shape)]
    args = [x2, norm_g, w[0], bd, gq, gk]
    for tab in rope_tabs[:2 if any(permuted) else 1]:
        in_specs += [rope_spec] * 3
        args += list(tab)
    if any(permuted):
        in_specs.append(_const_spec(perm.shape))
        args.append(perm)
    out_shapes, out_specs = [], []
    for _ in permuted:
        for wd in (BRANCH, kw, kw, kw, kw):
            out_shapes.append(jax.ShapeDtypeStruct((n, wd), bf16))
            out_specs.append(row_spec(wd))
    out_shapes.append(jax.ShapeDtypeStruct((n, BRANCH), bf16))
    out_specs.append(row_spec(BRANCH))
    return pl.pallas_call(
        functools.partial(_proj_kernel, permuted=permuted, n_kv=n_kv),
        grid=(n // rows,),
        in_specs=in_specs,
        out_specs=out_specs,
        out_shape=out_shapes,
        compiler_params=pltpu.CompilerParams(
            dimension_semantics=("parallel",), vmem_limit_bytes=VMEM_LIMIT),
        name="proj",
    )(*args)


def _slab_attention(q_tiles, k2, ks2, v2, vs2, eye, bias_t, mlo, mhi, sinks):
    npair = len(q_tiles) // 2
    lane_lo = lax.broadcasted_iota(jnp.int32, (BLOCK, LANES), 1) < HEAD_DIM
    variants = ((k2 * mlo, ks2 * mhi, v2 * mlo, vs2 * mhi),
                (ks2 * mlo, k2 * mhi, vs2 * mlo, v2 * mhi))
    results = []
    for kv in range(2):
        k_lo, k_hi, v_lo, v_hi = variants[kv]
        kx = (jnp.concatenate([k_lo, bias_t], axis=1), jnp.concatenate([k_hi, bias_t], axis=1))
        vx = (jnp.concatenate([v_lo, mlo], axis=1), jnp.concatenate([v_hi, mhi], axis=1))
        tiles = q_tiles[kv * npair:(kv + 1) * npair]
        qs = jnp.concatenate([jnp.concatenate([t, eye], axis=1) for t in tiles], axis=0)
        acc = None
        m_par, x_par = [], []
        for e in range(2):
            s_all = lax.dot_general(qs, kx[e], (((1,), (1,)), ((), ())), preferred_element_type=f32)
            ps, ms, xs = [], [], []
            for i in range(npair):
                s = s_all[BLOCK * i:BLOCK * (i + 1)].astype(bf16)
                mb = jnp.max(s, axis=-1, keepdims=True)
                if sinks is not None:
                    sk = sinks[kv * npair + i][e]
                    mb = jnp.maximum(mb.astype(f32), sk).astype(bf16)
                m = mb.astype(f32)
                if sinks is not None:
                    xs.append(jnp.exp2(sk - m))
                ps.append(jnp.exp2(s - mb))
                ms.append(m)
            ol = jnp.dot(jnp.concatenate(ps, axis=0), vx[e], preferred_element_type=f32)
            acc = ol if acc is None else acc + ol
            m_par.append(ms)
            x_par.append(xs)
        for i in range(npair):
            a = acc[BLOCK * i:BLOCK * (i + 1)]
            den = a[:, LANES:]
            if sinks is not None:
                den = den + jnp.where(lane_lo, x_par[0][i], x_par[1][i])
            o = a[:, :LANES] / den
            lse2 = jnp.where(lane_lo, m_par[0][i], m_par[1][i]) + jnp.log2(den)
            results.append((o, lse2))
    return results


def _compact_lse(lses):
    lane = lax.broadcasted_iota(jnp.int32, (BLOCK, LANES), 1) % HEAD_DIM
    out = jnp.zeros((BLOCK, LANES), f32)
    for j, l in enumerate(lses):
        out = jnp.where(lane == j, l, out)
    return out


def _attn_dense_kernel(*refs, n_kv, gated):
    (q_ref, kp_ref, kc_ref, ksp_ref, ksc_ref, vp_ref, vc_ref, vsp_ref, vsc_ref,
     eye_ref, bias_ref, mlo_ref, mhi_ref) = refs[:13]
    if gated:
        sink_ref, sg_ref, x_ref, wo_ref, xo_ref, y_buf = refs[13:]
    else:
        o_ref, lse_ref = refs[13:]
    n_slab = n_kv // 2
    tiles_per_slab = Q_TILES // n_slab
    eye, mlo, mhi = eye_ref[...], mlo_ref[...], mhi_ref[...]
    first_step = pl.program_id(1) == 0

    def keys(p_ref, c_ref, i, ls):
        if i == 0:
            return jnp.concatenate([p_ref[:, ls], c_ref[0:BLOCK, ls]], axis=0)
        return c_ref[BLOCK * (i - 1):BLOCK * (i + 1), ls]

    for i in range(ATT_ROWS // BLOCK):
        rq = slice(BLOCK * i, BLOCK * (i + 1))
        bias_t = bias_ref[jnp.where(first_step, 0, 1)] if i == 0 else bias_ref[1]
        for sl in range(n_slab):
            ls = slice(LANES * sl, LANES * (sl + 1))
            t0 = sl * tiles_per_slab
            q_tiles = [q_ref[rq, LANES * j:LANES * (j + 1)] for j in range(t0, t0 + tiles_per_slab)]
            sinks = None
            if gated:
                sinks = [(sink_ref[2 * j], sink_ref[2 * j + 1]) for j in range(t0, t0 + tiles_per_slab)]
            res = _slab_attention(q_tiles, keys(kp_ref, kc_ref, i, ls), keys(ksp_ref, ksc_ref, i, ls),
                                  keys(vp_ref, vc_ref, i, ls), keys(vsp_ref, vsc_ref, i, ls),
                                  eye, bias_t, mlo, mhi, sinks)
            for j, (o, lse2) in enumerate(res):
                cs = slice(LANES * (t0 + j), LANES * (t0 + j + 1))
                if gated:
                    y_buf[rq, cs] = (o * sg_ref[rq, cs].astype(f32)).astype(bf16)
                else:
                    o_ref[rq, cs] = o.astype(bf16)
            if not gated:
                lse_ref[sl, rq, :] = _compact_lse([l for _, l in res])
    if gated:
        y = y_buf[...]
        for c in range(0, D_MODEL, WIDE):
            xo_ref[:, c:c + WIDE] = x_ref[:, c:c + WIDE] + jnp.dot(
                y, wo_ref[:, c:c + WIDE], preferred_element_type=f32)


def _attention_dense(q, k, ksw, v, vsw, consts, *, batch, seq, n_kv, gate_args=None):
    n = batch * seq
    kw = n_kv * HEAD_DIM
    gated = gate_args is not None
    per_seq = seq // ATT_ROWS
    ratio = ATT_ROWS // BLOCK
    cur = lambda wd: pl.BlockSpec((ATT_ROWS, wd), lambda b, j: (b * per_seq + j, 0))
    prev = lambda wd: pl.BlockSpec(
        (BLOCK, wd), lambda b, j: ((b * per_seq + j) * ratio - jnp.minimum(j, 1), 0))
    eye, bias, mlo, mhi = consts
    in_specs = [cur(BRANCH), prev(kw), cur(kw), prev(kw), cur(kw), prev(kw), cur(kw), prev(kw), cur(kw),
                _const_spec(eye.shape), _const_spec(bias.shape), _const_spec(mlo.shape),
                _const_spec(mhi.shape)]
    args = [q, k, k, ksw, ksw, v, v, vsw, vsw, eye, bias, mlo, mhi]
    scratch = []
    if gated:
        sinks, sg, x2, w_out = gate_args
        in_specs += [pl.BlockSpec(memory_space=pltpu.SMEM), cur(BRANCH), cur(D_MODEL),
                     _layer_spec(w_out)]
        args += [sinks, sg, x2, w_out[0]]
        out_shape = jax.ShapeDtypeStruct((n, D_MODEL), f32)
        out_specs = cur(D_MODEL)
        scratch = scratch + [pltpu.VMEM((ATT_ROWS, BRANCH), bf16)]
    else:
        out_shape = [jax.ShapeDtypeStruct((n, BRANCH), bf16),
                     jax.ShapeDtypeStruct((n_kv // 2, n, LANES), f32)]
        out_specs = [cur(BRANCH),
                     pl.BlockSpec((n_kv // 2, ATT_ROWS, LANES), lambda b, j: (0, b * per_seq + j, 0))]
    return pl.pallas_call(
        functools.partial(_attn_dense_kernel, n_kv=n_kv, gated=gated),
        grid=(batch, per_seq),
        in_specs=in_specs,
        out_specs=out_specs,
        out_shape=out_shape,
        scratch_shapes=scratch,
        compiler_params=pltpu.CompilerParams(
            dimension_semantics=("parallel", "arbitrary"), vmem_limit_bytes=VMEM_LIMIT),
        name="attn_dense_a" if gated else "attn_dense_b",
    )(*args)


def _attn_runs_kernel(q_ref, kp_ref, kc_ref, ksp_ref, ksc_ref, vp_ref, vc_ref, vsp_ref, vsc_ref,
                      eye_ref, bias_ref, mlo_ref, mhi_ref, o_ref, lse_ref, *, dilation, n_kv):
    eye, mlo, mhi = eye_ref[...], mlo_ref[...], mhi_ref[...]
    n_slab = n_kv // 2
    tiles_per_slab = Q_TILES // n_slab
    chunk = BLOCK * dilation
    run_pitch = chunk // N_RUNS

    def residue(c, h):
        def runs_at(base):
            if isinstance(c, int):
                return [slice(base + i * run_pitch + c * RUN, base + i * run_pitch + (c + 1) * RUN)
                        for i in range(N_RUNS)]
            return [pl.ds(pl.multiple_of(base + i * run_pitch + c * RUN, RUN), RUN) for i in range(N_RUNS)]
        cur_runs = runs_at(h * chunk)
        gather = lambda ref, runs, ls: jnp.concatenate([ref[r, ls] for r in runs], axis=0)
        if h == 0:
            both = lambda p_ref, c_ref, ls: jnp.concatenate(
                [gather(p_ref, runs_at(0), ls), gather(c_ref, cur_runs, ls)], axis=0)
            bias_t = bias_ref[jnp.minimum(pl.program_id(1), 1)]
        else:
            both = lambda p_ref, c_ref, ls: jnp.concatenate(
                [gather(c_ref, runs_at((h - 1) * chunk), ls), gather(c_ref, cur_runs, ls)], axis=0)
            bias_t = bias_ref[1]
        for sl in range(n_slab):
            ls = slice(LANES * sl, LANES * (sl + 1))
            t0 = sl * tiles_per_slab
            q_tiles = [gather(q_ref, cur_runs, slice(LANES * j, LANES * (j + 1)))
                       for j in range(t0, t0 + tiles_per_slab)]
            res = _slab_attention(q_tiles, both(kp_ref, kc_ref, ls), both(ksp_ref, ksc_ref, ls),
                                  both(vp_ref, vc_ref, ls), both(vsp_ref, vsc_ref, ls),
                                  eye, bias_t, mlo, mhi, None)
            lse = _compact_lse([l for _, l in res])
            for i, r in enumerate(cur_runs):
                rr = slice(RUN * i, RUN * (i + 1))
                for j, (o, _) in enumerate(res):
                    o_ref[r, LANES * (t0 + j):LANES * (t0 + j + 1)] = o[rr].astype(bf16)
                lse_ref[sl, r, :] = lse[rr]

    n_chunks = q_ref.shape[0] // chunk
    if n_chunks * dilation <= ATT_ROWS // BLOCK:
        for h in range(n_chunks):
            for c in range(dilation):
                residue(c, h)
    else:
        for h in range(n_chunks):
            def body(c, carry, h=h):
                residue(c, h)
                return carry
            lax.fori_loop(0, dilation, body, 0, unroll=ATT_ROWS // BLOCK)


def _attention_runs(q, k, ksw, v, vsw, consts, *, batch, seq, dilation, n_kv):
    n = batch * seq
    kw = n_kv * HEAD_DIM
    chunk = BLOCK * dilation
    rows = max(chunk, ATT_ROWS)
    per_seq = seq // rows
    ratio = rows // chunk
    cur = lambda wd: pl.BlockSpec((rows, wd), lambda b, j: (b * per_seq + j, 0))
    prev = lambda wd: pl.BlockSpec(
        (chunk, wd), lambda b, j: ((b * per_seq + j) * ratio - jnp.minimum(j, 1), 0))
    eye, bias, mlo, mhi = consts
    return pl.pallas_call(
        functools.partial(_attn_runs_kernel, dilation=dilation, n_kv=n_kv),
        grid=(batch, per_seq),
        in_specs=[cur(BRANCH), prev(kw), cur(kw), prev(kw), cur(kw), prev(kw), cur(kw), prev(kw), cur(kw),
                  _const_spec(eye.shape), _const_spec(bias.shape), _const_spec(mlo.shape),
                  _const_spec(mhi.shape)],
        out_specs=[cur(BRANCH),
                   pl.BlockSpec((n_kv // 2, rows, LANES), lambda b, j: (0, b * per_seq + j, 0))],
        out_shape=[jax.ShapeDtypeStruct((n, BRANCH), bf16),
                   jax.ShapeDtypeStruct((n_kv // 2, n, LANES), f32)],
        compiler_params=pltpu.CompilerParams(
            dimension_semantics=("parallel", "arbitrary"), vmem_limit_bytes=VMEM_LIMIT),
        name=f"attn_runs_d{dilation}",
    )(q, k, k, ksw, ksw, v, v, vsw, vsw, eye, bias, mlo, mhi)


def _merge_kernel(x_ref, o0_ref, l0_ref, o1_ref, l1_ref, o2_ref, l2_ref, sg_ref, ex_ref, unperm_ref,
                  w_ref, o_ref, lnat):
    n_slab = l0_ref.shape[0]
    o12 = []
    for t0 in range(0, x_ref.shape[0], TILE):
        rt = slice(t0, t0 + TILE)
        o12.append(jnp.dot(unperm_ref[...], jnp.concatenate([o1_ref[rt, :], o2_ref[rt, :]], axis=1),
                           preferred_element_type=f32))
        for gi, l_ref in enumerate((l1_ref, l2_ref)):
            for s in range(n_slab):
                for c in range(MAX_DIL):
                    lnat[gi * n_slab + s, pl.ds(t0 + c, RUN, stride=MAX_DIL), :] = (
                        l_ref[s, t0 + RUN * c:t0 + RUN * (c + 1), :])
    o12 = jnp.concatenate(o12, axis=0)
    ex = ex_ref[...]
    wts = [[], []]
    for s in range(n_slab):
        l0, l1, l2 = l0_ref[s], lnat[s], lnat[n_slab + s]
        mx = jnp.maximum(jnp.maximum(l0, l1), l2)
        e = [jnp.exp2(l0 - mx), jnp.exp2(l1 - mx), jnp.exp2(l2 - mx)]
        inv = 1.0 / (e[0] + e[1] + e[2])
        for g in range(2):
            wg = e[g] * inv
            hi = wg.astype(bf16)
            lo = (wg - hi.astype(f32)).astype(bf16)
            wts[g].append(jnp.dot(jnp.concatenate([hi, lo], axis=1), ex, preferred_element_type=f32))
    w0, w1 = (jnp.concatenate(w, axis=1) for w in wts)
    w2 = 1.0 - w0 - w1
    y = ((w0 * o0_ref[...].astype(f32) + w1 * o12[:, :BRANCH] + w2 * o12[:, BRANCH:])
         * sg_ref[...].astype(f32)).astype(bf16)
    for c in range(0, D_MODEL, WIDE):
        o_ref[:, c:c + WIDE] = x_ref[:, c:c + WIDE] + jnp.dot(
            y, w_ref[:, c:c + WIDE], preferred_element_type=f32)


def _merge_project(x2, o0, l0, o1, l1, o2, l2, sg, ex, unperm, w):
    n = x2.shape[0]
    rows = MERGE_TILES * TILE
    row = pl.BlockSpec((rows, D_MODEL), lambda i: (i, 0))
    slab = pl.BlockSpec((l0.shape[0], rows, LANES), lambda i: (0, i, 0))
    return pl.pallas_call(
        _merge_kernel,
        grid=(n // rows,),
        in_specs=[row, row, slab, row, slab, row, slab, row, _const_spec(ex.shape),
                  _const_spec(unperm.shape), _layer_spec(w)],
        out_specs=row,
        out_shape=jax.ShapeDtypeStruct((n, D_MODEL), f32),
        scratch_shapes=[pltpu.VMEM((2 * l0.shape[0], rows, LANES), f32)],
        input_output_aliases={0: 0},
        compiler_params=pltpu.CompilerParams(
            dimension_semantics=("parallel",), vmem_limit_bytes=VMEM_LIMIT),
        name="merge_outproj",
    )(x2, o0, l0, o1, l1, o2, l2, sg, ex, unperm, w[0])


def _rope_tables(seq):
    pos = np.arange(seq, dtype=np.float64)
    inv = ROPE_THETA ** (-np.arange(0, ROT_DIM, 2, dtype=np.float64) / ROT_DIM)
    ang = pos[:, None] * inv[None, :]
    cos, sin = np.cos(ang), np.sin(ang)
    half = ROT_DIM // 2
    pad = HEAD_DIM - ROT_DIM
    one = np.ones((seq, pad))
    zero = np.zeros((seq, pad))
    zh = np.zeros((seq, half))
    c = np.concatenate([cos, cos, one], axis=1)
    s1 = np.concatenate([-sin, zh, zero], axis=1)
    s2 = np.concatenate([zh, sin, zero], axis=1)
    tile = lambda a: np.tile(a, (1, LANES // HEAD_DIM)).astype(np.float32)
    return tile(c), tile(s1), tile(s2)


def _bias_tables(max_dist, order):
    qi = order[None, :]
    kj = np.concatenate([order, order + BLOCK])[:, None]
    dist = BLOCK + qi - kj
    ok = (dist >= 0) & (dist <= max_dist)
    first = ok & (kj >= BLOCK)
    return np.stack([np.where(first, 0.0, NEG), np.where(ok, 0.0, NEG)]).astype(bf16)


def kernel(x, norm_a, w_in_a, q_gain_a, k_gain_a, sinks_a, w_out_a,
           norm_b, w_in_b, q_gain_b, k_gain_b, w_out_b):
    batch, seq, _ = x.shape
    n = batch * seq
    hid = np.arange(CHUNK) // HEAD_DIM
    bd = np.where(hid[:, None] == hid[None, :], 1.0 / HEAD_DIM, 0.0).astype(bf16)
    lane = np.arange(LANES)[None, :]
    mlo = np.broadcast_to(lane < HEAD_DIM, (2 * BLOCK, LANES)).astype(bf16)
    mhi = np.broadcast_to(lane >= HEAD_DIM, (2 * BLOCK, LANES)).astype(bf16)
    eye = np.eye(BLOCK).astype(bf16)
    natural = np.arange(BLOCK)
    consts_a = (eye, _bias_tables(A_WINDOW - 1, natural), mlo, mhi)

    r = np.arange(TILE)
    dest = RUN * (r % MAX_DIL) + r // MAX_DIL
    perm = (np.arange(TILE)[:, None] == dest[None, :]).astype(bf16)
    unperm = np.ascontiguousarray(perm.T)
    src = MAX_DIL * (r % RUN) + r // RUN
    rope_nat = _rope_tables(seq)
    rope_perm = tuple(t.reshape(seq // TILE, TILE, LANES)[:, src, :].reshape(seq, LANES) for t in rope_nat)
    order = {}
    for _, d in B_PAIRS:
        if d == 1 or d == MAX_DIL:
            order[d] = natural
        else:
            assert BLOCK * d == TILE and MAX_DIL % d == 0
            step = MAX_DIL // d
            order[d] = step * (natural % RUN) + natural // RUN
    consts_b = {d: (eye, _bias_tables(BLOCK, order[d]), mlo, mhi) for _, d in B_PAIRS}

    half = BRANCH // (B_KV // 2)
    col_tile = np.arange(half)[None, :] // LANES
    col_odd = (np.arange(half)[None, :] % LANES) // HEAD_DIM
    row = np.arange(LANES)[:, None]
    ex = ((row % HEAD_DIM == col_tile) & (row // HEAD_DIM == col_odd)).astype(bf16)
    ex = np.concatenate([ex, ex], axis=0)
    tile_gain = lambda g: jnp.tile(g, (1, CHUNK // HEAD_DIM))[:, None, :]
    q_scale = SCALE * LOG2E
    w_in_a, w_out_a, w_in_b, w_out_b = (w.astype(bf16) for w in (w_in_a, w_out_a, w_in_b, w_out_b))

    x2 = x.reshape(n, D_MODEL)
    depth = norm_a.shape[0] + norm_b.shape[0]
    for layer in range(depth):
        idx = layer // 2
        if layer % 2 == 0:
            q, k, ksw, v, vsw, sg = _project(
                x2, norm_a[idx][None], (w_in_a, idx), bd,
                tile_gain(q_gain_a[idx][None] * q_scale), tile_gain(k_gain_a[idx][None]),
                (rope_nat,), None, permuted=(False,), n_kv=A_KV, seq=seq)
            x2 = _attention_dense(
                q, k, ksw, v, vsw, consts_a, batch=batch, seq=seq, n_kv=A_KV,
                gate_args=(sinks_a[idx].astype(f32) * LOG2E, sg, x2, (w_out_a, idx)))
        else:
            permuted = tuple(d > 1 for _, d in B_PAIRS)
            outs = _project(
                x2, norm_b[idx][None], (w_in_b, idx), bd,
                tile_gain(q_gain_b[idx] * q_scale), tile_gain(k_gain_b[idx]),
                (rope_nat, rope_perm), perm, permuted=permuted, n_kv=B_KV, seq=seq)
            sg = outs[-1]
            merged = []
            for g, (window, dilation) in enumerate(B_PAIRS):
                assert window // dilation == BLOCK
                qkv = outs[5 * g:5 * g + 5]
                if dilation == 1:
                    merged += _attention_dense(*qkv, consts_b[1], batch=batch, seq=seq, n_kv=B_KV)
                else:
                    merged += _attention_runs(*qkv, consts_b[dilation], batch=batch, seq=seq,
                                              dilation=dilation, n_kv=B_KV)
            x2 = _merge_project(x2, *merged, sg, ex, unperm, (w_out_b, idx))
    return x2.reshape(batch, seq, D_MODEL)
```

```python
import functools
import math

import jax
import jax.numpy as jnp
import numpy as np
from jax import lax
from jax.experimental import pallas as pl
from jax.experimental.pallas import tpu as pltpu

D_MODEL = 1024
HEAD_DIM = 64
N_HEADS = 16
BRANCH = N_HEADS * HEAD_DIM
A_KV = 2
B_KV = 4
B_PAIRS = ((128, 1), (512, 4), (2048, 16))
A_WINDOW = 128
BLOCK = 128
ROT_DIM = HEAD_DIM // 4
ROPE_THETA = 500000.0
EPS = 1e-6
SCALE = HEAD_DIM ** -0.5
LOG2E = math.log2(math.e)
LANES = 128
Q_TILES = BRANCH // LANES
NEG = -1e30

TILE = 512
ATT_ROWS = 1024
ATT_ROWS_B = 2048
STATIC_BLOCKS = 16
MERGE_TILES = 2
CHUNK = 256
WIDE = 512
MAX_DIL = 16
RUN = TILE // MAX_DIL
N_RUNS = BLOCK // RUN
VMEM_LIMIT = 56 * 1024 * 1024

bf16 = jnp.bfloat16
f32 = jnp.float32


def _const_spec(shape):
    nd = len(shape)
    return pl.BlockSpec(shape, lambda *_: (0,) * nd, pipeline_mode=pl.Buffered(1))


def _layer_spec(stacked_and_index):
    stacked, index = stacked_and_index
    return pl.BlockSpec((None,) + stacked.shape[1:], lambda *_: (index, 0, 0), pipeline_mode=pl.Buffered(1))


def _lane_tiles(z):
    return [z[:, i:i + LANES] for i in range(0, z.shape[1], LANES)]


def _proj_kernel(*refs, permuted, n_kv):
    x_ref, ng_ref, w_ref, bd_ref, gq_ref, gk_ref = refs[:6]
    n_tab = 2 if any(permuted) else 1
    tabs = [tuple(r[...] for r in refs[6 + 3 * i:9 + 3 * i]) for i in range(n_tab)]
    pos = 6 + 3 * n_tab
    if any(permuted):
        perm_ref = refs[pos]
        pos += 1
    out_refs = refs[pos:]

    x = x_ref[...]
    ms = jnp.mean(x * x, axis=-1, keepdims=True)
    h = ((x * lax.rsqrt(ms + EPS)) * ng_ref[...]).astype(bf16)
    hs = [h]
    if any(permuted):
        hs.append(jnp.dot(perm_ref[...], h, preferred_element_type=f32).astype(bf16))
    bd = bd_ref[...]

    def roll_tiles(t, shift):
        return jnp.concatenate([pltpu.roll(a, shift, axis=1) for a in _lane_tiles(t)], axis=1)

    def head_norm_rope(z, gain, tab):
        w = z.shape[1]
        cos, sn1, sn2 = (jnp.concatenate([a] * (w // LANES), axis=1) for a in tab)
        msq = jnp.dot((z * z).astype(bf16), bd[:w, :w], preferred_element_type=f32)
        t = (z * lax.rsqrt(msq + EPS)) * gain[:, :w]
        up = roll_tiles(t, LANES - ROT_DIM // 2)
        dn = roll_tiles(t, ROT_DIM // 2)
        return t * cos + up * sn1 + dn * sn2

    kw = n_kv * HEAD_DIM
    col = 0
    oi = 0
    for g, perm in enumerate(permuted):
        hg, tab = hs[int(perm)], tabs[int(perm)]
        q_ref, k_ref, ksw_ref, v_ref, vsw_ref = out_refs[oi:oi + 5]
        oi += 5
        for c in range(0, BRANCH, WIDE):
            zz = jnp.dot(hg, w_ref[:, col + c:col + c + WIDE], preferred_element_type=f32)
            for cc in range(0, WIDE, CHUNK):
                q_ref[:, c + cc:c + cc + CHUNK] = head_norm_rope(
                    zz[:, cc:cc + CHUNK], gq_ref[g], tab).astype(bf16)
        col += BRANCH
        zz = jnp.dot(hg, w_ref[:, col:col + 2 * kw], preferred_element_type=f32)
        k = head_norm_rope(zz[:, :kw], gk_ref[g], tab)
        k_ref[...] = k.astype(bf16)
        ksw_ref[...] = roll_tiles(k, LANES // 2).astype(bf16)
        v = zz[:, kw:]
        v_ref[...] = v.astype(bf16)
        vsw_ref[...] = roll_tiles(v, LANES // 2).astype(bf16)
        col += 2 * kw
    sg_ref = out_refs[oi]
    for c in range(0, BRANCH, WIDE):
        gt = jnp.dot(h, w_ref[:, col + c:col + c + WIDE], preferred_element_type=f32)
        sg_ref[:, c:c + WIDE] = (gt * (1.0 / (1.0 + jnp.exp(-gt)))).astype(bf16)


def _project(x2, norm_g, w, bd, gq, gk, rope_tabs, perm, *, permuted, n_kv, seq):
    n = x2.shape[0]
    kw = n_kv * HEAD_DIM
    rows = TILE if any(permuted) else 2 * TILE
    per_seq = seq // rows
    row_spec = lambda wd: pl.BlockSpec((rows, wd), lambda i: (i, 0))
    rope_spec = pl.BlockSpec((rows, LANES), lambda i: (i % per_seq, 0))
    in_specs = [row_spec(D_MODEL), _const_spec((1, D_MODEL)), _layer_spec(w),
                _const_spec((CHUNK, CHUNK)), _const_spec(gq.shape), _const_spec(gk.shape)]
    args = [x2, norm_g, w[0], bd, gq, gk]
    for tab in rope_tabs[:2 if any(permuted) else 1]:
        in_specs += [rope_spec] * 3
        args += list(tab)
    if any(permuted):
        in_specs.append(_const_spec(perm.shape))
        args.append(perm)
    out_shapes, out_specs = [], []
    for _ in permuted:
        for wd in (BRANCH, kw, kw, kw, kw):
            out_shapes.append(jax.ShapeDtypeStruct((n, wd), bf16))
            out_specs.append(row_spec(wd))
    out_shapes.append(jax.ShapeDtypeStruct((n, BRANCH), bf16))
    out_specs.append(row_spec(BRANCH))
    return pl.pallas_call(
        functools.partial(_proj_kernel, permuted=permuted, n_kv=n_kv),
        grid=(n // rows,),
        in_specs=in_specs,
        out_specs=out_specs,
        out_shape=out_shapes,
        compiler_params=pltpu.CompilerParams(
            dimension_semantics=("parallel",), vmem_limit_bytes=VMEM_LIMIT),
        name="proj",
    )(*args)


def _slab_attention(q_tiles, k2, ks2, v2, vs2, eye, bias_t, mlo, mhi, sinks):
    npair = len(q_tiles) // 2
    lane_lo = lax.broadcasted_iota(jnp.int32, (BLOCK, LANES), 1) < HEAD_DIM
    variants = ((k2 * mlo, ks2 * mhi, v2 * mlo, vs2 * mhi),
                (ks2 * mlo, k2 * mhi, vs2 * mlo, v2 * mhi))
    results = []
    for kv in range(2):
        k_lo, k_hi, v_lo, v_hi = variants[kv]
        kx = (jnp.concatenate([k_lo, bias_t], axis=1), jnp.concatenate([k_hi, bias_t], axis=1))
        vx = (jnp.concatenate([v_lo, mlo], axis=1), jnp.concatenate([v_hi, mhi], axis=1))
        tiles = q_tiles[kv * npair:(kv + 1) * npair]
        qs = jnp.concatenate([jnp.concatenate([t, eye], axis=1) for t in tiles], axis=0)
        acc = None
        m_par, x_par = [], []
        for e in range(2):
            s_all = lax.dot_general(qs, kx[e], (((1,), (1,)), ((), ())), preferred_element_type=f32)
            ps, ms, xs = [], [], []
            for i in range(npair):
                s = s_all[BLOCK * i:BLOCK * (i + 1)].astype(bf16)
                mb = jnp.max(s, axis=-1, keepdims=True)
                if sinks is not None:
                    sk = sinks[kv * npair + i][e]
                    mb = jnp.maximum(mb.astype(f32), sk).astype(bf16)
                m = mb.astype(f32)
                if sinks is not None:
                    xs.append(jnp.exp2(sk - m))
                ps.append(jnp.exp2(s - mb))
                ms.append(m)
            ol = jnp.dot(jnp.concatenate(ps, axis=0), vx[e], preferred_element_type=f32)
            acc = ol if acc is None else acc + ol
            m_par.append(ms)
            x_par.append(xs)
        for i in range(npair):
            a = acc[BLOCK * i:BLOCK * (i + 1)]
            den = a[:, LANES:]
            if sinks is not None:
                den = den + jnp.where(lane_lo, x_par[0][i], x_par[1][i])
            o = a[:, :LANES] / den
            lse2 = jnp.where(lane_lo, m_par[0][i], m_par[1][i]) + jnp.log2(den)
            results.append((o, lse2))
    return results


def _compact_lse(lses):
    lane = lax.broadcasted_iota(jnp.int32, (BLOCK, LANES), 1) % HEAD_DIM
    out = jnp.zeros((BLOCK, LANES), f32)
    for j, l in enumerate(lses):
        out = jnp.where(lane == j, l, out)
    return out


def _attn_dense_kernel(*refs, n_kv, gated):
    (q_ref, kp_ref, kc_ref, ksp_ref, ksc_ref, vp_ref, vc_ref, vsp_ref, vsc_ref,
     eye_ref, bias_ref, mlo_ref, mhi_ref) = refs[:13]
    if gated:
        sink_ref, sg_ref, x_ref, wo_ref, xo_ref, y_buf = refs[13:]
    else:
        o_ref, lse_ref = refs[13:]
    n_slab = n_kv // 2
    tiles_per_slab = Q_TILES // n_slab
    eye, mlo, mhi = eye_ref[...], mlo_ref[...], mhi_ref[...]
    first_step = pl.program_id(1) == 0

    def keys(p_ref, c_ref, i, ls):
        if i == 0:
            return jnp.concatenate([p_ref[:, ls], c_ref[0:BLOCK, ls]], axis=0)
        return c_ref[BLOCK * (i - 1):BLOCK * (i + 1), ls]

    for i in range(q_ref.shape[0] // BLOCK):
        rq = slice(BLOCK * i, BLOCK * (i + 1))
        bias_t = bias_ref[jnp.where(first_step, 0, 1)] if i == 0 else bias_ref[1]
        for sl in range(n_slab):
            ls = slice(LANES * sl, LANES * (sl + 1))
            t0 = sl * tiles_per_slab
            q_tiles = [q_ref[rq, LANES * j:LANES * (j + 1)] for j in range(t0, t0 + tiles_per_slab)]
            sinks = None
            if gated:
                sinks = [(sink_ref[2 * j], sink_ref[2 * j + 1]) for j in range(t0, t0 + tiles_per_slab)]
            res = _slab_attention(q_tiles, keys(kp_ref, kc_ref, i, ls), keys(ksp_ref, ksc_ref, i, ls),
                                  keys(vp_ref, vc_ref, i, ls), keys(vsp_ref, vsc_ref, i, ls),
                                  eye, bias_t, mlo, mhi, sinks)
            for j, (o, lse2) in enumerate(res):
                cs = slice(LANES * (t0 + j), LANES * (t0 + j + 1))
                if gated:
                    y_buf[rq, cs] = (o * sg_ref[rq, cs].astype(f32)).astype(bf16)
                else:
                    o_ref[rq, cs] = o.astype(bf16)
            if not gated:
                lse_ref[sl, rq, :] = _compact_lse([l for _, l in res])
    if gated:
        y = y_buf[...]
        for c in range(0, D_MODEL, WIDE):
            xo_ref[:, c:c + WIDE] = x_ref[:, c:c + WIDE] + jnp.dot(
                y, wo_ref[:, c:c + WIDE], preferred_element_type=f32)


def _attention_dense(q, k, ksw, v, vsw, consts, *, batch, seq, n_kv, gate_args=None):
    n = batch * seq
    kw = n_kv * HEAD_DIM
    gated = gate_args is not None
    rows = ATT_ROWS if gated else ATT_ROWS_B
    per_seq = seq // rows
    ratio = rows // BLOCK
    cur = lambda wd: pl.BlockSpec((rows, wd), lambda b, j: (b * per_seq + j, 0))
    prev = lambda wd: pl.BlockSpec(
        (BLOCK, wd), lambda b, j: ((b * per_seq + j) * ratio - jnp.minimum(j, 1), 0))
    eye, bias, mlo, mhi = consts
    in_specs = [cur(BRANCH), prev(kw), cur(kw), prev(kw), cur(kw), prev(kw), cur(kw), prev(kw), cur(kw),
                _const_spec(eye.shape), _const_spec(bias.shape), _const_spec(mlo.shape),
                _const_spec(mhi.shape)]
    args = [q, k, k, ksw, ksw, v, v, vsw, vsw, eye, bias, mlo, mhi]
    scratch = []
    if gated:
        sinks, sg, x2, w_out = gate_args
        in_specs += [pl.BlockSpec(memory_space=pltpu.SMEM), cur(BRANCH), cur(D_MODEL),
                     _layer_spec(w_out)]
        args += [sinks, sg, x2, w_out[0]]
        out_shape = jax.ShapeDtypeStruct((n, D_MODEL), f32)
        out_specs = cur(D_MODEL)
        scratch = scratch + [pltpu.VMEM((rows, BRANCH), bf16)]
    else:
        out_shape = [jax.ShapeDtypeStruct((n, BRANCH), bf16),
                     jax.ShapeDtypeStruct((n_kv // 2, n, LANES), f32)]
        out_specs = [cur(BRANCH),
                     pl.BlockSpec((n_kv // 2, rows, LANES), lambda b, j: (0, b * per_seq + j, 0))]
    return pl.pallas_call(
        functools.partial(_attn_dense_kernel, n_kv=n_kv, gated=gated),
        grid=(batch, per_seq),
        in_specs=in_specs,
        out_specs=out_specs,
        out_shape=out_shape,
        scratch_shapes=scratch,
        compiler_params=pltpu.CompilerParams(
            dimension_semantics=("parallel", "arbitrary"), vmem_limit_bytes=VMEM_LIMIT),
        name="attn_dense_a" if gated else "attn_dense_b",
    )(*args)


def _attn_runs_kernel(q_ref, kp_ref, kc_ref, ksp_ref, ksc_ref, vp_ref, vc_ref, vsp_ref, vsc_ref,
                      eye_ref, bias_ref, mlo_ref, mhi_ref, o_ref, lse_ref, *, dilation, n_kv):
    eye, mlo, mhi = eye_ref[...], mlo_ref[...], mhi_ref[...]
    n_slab = n_kv // 2
    tiles_per_slab = Q_TILES // n_slab
    chunk = BLOCK * dilation
    run_pitch = chunk // N_RUNS

    def residue(c, h):
        def runs_at(base):
            if isinstance(c, int):
                return [slice(base + i * run_pitch + c * RUN, base + i * run_pitch + (c + 1) * RUN)
                        for i in range(N_RUNS)]
            return [pl.ds(pl.multiple_of(base + i * run_pitch + c * RUN, RUN), RUN) for i in range(N_RUNS)]
        cur_runs = runs_at(h * chunk)
        gather = lambda ref, runs, ls: jnp.concatenate([ref[r, ls] for r in runs], axis=0)
        if h == 0:
            both = lambda p_ref, c_ref, ls: jnp.concatenate(
                [gather(p_ref, runs_at(0), ls), gather(c_ref, cur_runs, ls)], axis=0)
            bias_t = bias_ref[jnp.minimum(pl.program_id(1), 1)]
        else:
            both = lambda p_ref, c_ref, ls: jnp.concatenate(
                [gather(c_ref, runs_at((h - 1) * chunk), ls), gather(c_ref, cur_runs, ls)], axis=0)
            bias_t = bias_ref[1]
        for sl in range(n_slab):
            ls = slice(LANES * sl, LANES * (sl + 1))
            t0 = sl * tiles_per_slab
            q_tiles = [gather(q_ref, cur_runs, slice(LANES * j, LANES * (j + 1)))
                       for j in range(t0, t0 + tiles_per_slab)]
            res = _slab_attention(q_tiles, both(kp_ref, kc_ref, ls), both(ksp_ref, ksc_ref, ls),
                                  both(vp_ref, vc_ref, ls), both(vsp_ref, vsc_ref, ls),
                                  eye, bias_t, mlo, mhi, None)
            lse = _compact_lse([l for _, l in res])
            for i, r in enumerate(cur_runs):
                rr = slice(RUN * i, RUN * (i + 1))
                for j, (o, _) in enumerate(res):
                    o_ref[r, LANES * (t0 + j):LANES * (t0 + j + 1)] = o[rr].astype(bf16)
                lse_ref[sl, r, :] = lse[rr]

    n_chunks = q_ref.shape[0] // chunk
    if n_chunks * dilation <= STATIC_BLOCKS:
        for h in range(n_chunks):
            for c in range(dilation):
                residue(c, h)
    else:
        for h in range(n_chunks):
            def body(c, carry, h=h):
                residue(c, h)
                return carry
            lax.fori_loop(0, dilation, body, 0, unroll=STATIC_BLOCKS)


def _attention_runs(q, k, ksw, v, vsw, consts, *, batch, seq, dilation, n_kv):
    n = batch * seq
    kw = n_kv * HEAD_DIM
    chunk = BLOCK * dilation
    rows = max(chunk, ATT_ROWS_B)
    per_seq = seq // rows
    ratio = rows // chunk
    cur = lambda wd: pl.BlockSpec((rows, wd), lambda b, j: (b * per_seq + j, 0))
    prev = lambda wd: pl.BlockSpec(
        (chunk, wd), lambda b, j: ((b * per_seq + j) * ratio - jnp.minimum(j, 1), 0))
    eye, bias, mlo, mhi = consts
    return pl.pallas_call(
        functools.partial(_attn_runs_kernel, dilation=dilation, n_kv=n_kv),
        grid=(batch, per_seq),
        in_specs=[cur(BRANCH), prev(kw), cur(kw), prev(kw), cur(kw), prev(kw), cur(kw), prev(kw), cur(kw),
                  _const_spec(eye.shape), _const_spec(bias.shape), _const_spec(mlo.shape),
                  _const_spec(mhi.shape)],
        out_specs=[cur(BRANCH),
                   pl.BlockSpec((n_kv // 2, rows, LANES), lambda b, j: (0, b * per_seq + j, 0))],
        out_shape=[jax.ShapeDtypeStruct((n, BRANCH), bf16),
                   jax.ShapeDtypeStruct((n_kv // 2, n, LANES), f32)],
        compiler_params=pltpu.CompilerParams(
            dimension_semantics=("parallel", "arbitrary"), vmem_limit_bytes=VMEM_LIMIT),
        name=f"attn_runs_d{dilation}",
    )(q, k, k, ksw, ksw, v, v, vsw, vsw, eye, bias, mlo, mhi)


def _merge_kernel(x_ref, o0_ref, l0_ref, o1_ref, l1_ref, o2_ref, l2_ref, sg_ref, ex_ref, unperm_ref,
                  w_ref, o_ref, lnat):
    n_slab = l0_ref.shape[0]
    o12 = []
    for t0 in range(0, x_ref.shape[0], TILE):
        rt = slice(t0, t0 + TILE)
        o12.append(jnp.dot(unperm_ref[...], jnp.concatenate([o1_ref[rt, :], o2_ref[rt, :]], axis=1),
                           preferred_element_type=f32))
        for gi, l_ref in enumerate((l1_ref, l2_ref)):
            for s in range(n_slab):
                for c in range(MAX_DIL):
                    lnat[gi * n_slab + s, pl.ds(t0 + c, RUN, stride=MAX_DIL), :] = (
                        l_ref[s, t0 + RUN * c:t0 + RUN * (c + 1), :])
    o12 = jnp.concatenate(o12, axis=0)
    ex = ex_ref[...]
    wts = [[], []]
    for s in range(n_slab):
        l0, l1, l2 = l0_ref[s], lnat[s], lnat[n_slab + s]
        mx = jnp.maximum(jnp.maximum(l0, l1), l2)
        e = [jnp.exp2(l0 - mx), jnp.exp2(l1 - mx), jnp.exp2(l2 - mx)]
        inv = 1.0 / (e[0] + e[1] + e[2])
        for g in range(2):
            wg = e[g] * inv
            hi = wg.astype(bf16)
            lo = (wg - hi.astype(f32)).astype(bf16)
            wts[g].append(jnp.dot(jnp.concatenate([hi, lo], axis=1), ex, preferred_element_type=f32))
    w0, w1 = (jnp.concatenate(w, axis=1) for w in wts)
    w2 = 1.0 - w0 - w1
    y = ((w0 * o0_ref[...].astype(f32) + w1 * o12[:, :BRANCH] + w2 * o12[:, BRANCH:])
         * sg_ref[...].astype(f32)).astype(bf16)
    for c in range(0, D_MODEL, WIDE):
        o_ref[:, c:c + WIDE] = x_ref[:, c:c + WIDE] + jnp.dot(
            y, w_ref[:, c:c + WIDE], preferred_element_type=f32)


def _merge_project(x2, o0, l0, o1, l1, o2, l2, sg, ex, unperm, w):
    n = x2.shape[0]
    rows = MERGE_TILES * TILE
    row = pl.BlockSpec((rows, D_MODEL), lambda i: (i, 0))
    slab = pl.BlockSpec((l0.shape[0], rows, LANES), lambda i: (0, i, 0))
    return pl.pallas_call(
        _merge_kernel,
        grid=(n // rows,),
        in_specs=[row, row, slab, row, slab, row, slab, row, _const_spec(ex.shape),
                  _const_spec(unperm.shape), _layer_spec(w)],
        out_specs=row,
        out_shape=jax.ShapeDtypeStruct((n, D_MODEL), f32),
        scratch_shapes=[pltpu.VMEM((2 * l0.shape[0], rows, LANES), f32)],
        input_output_aliases={0: 0},
        compiler_params=pltpu.CompilerParams(
            dimension_semantics=("parallel",), vmem_limit_bytes=VMEM_LIMIT),
        name="merge_outproj",
    )(x2, o0, l0, o1, l1, o2, l2, sg, ex, unperm, w[0])


def _rope_tables(seq):
    pos = np.arange(seq, dtype=np.float64)
    inv = ROPE_THETA ** (-np.arange(0, ROT_DIM, 2, dtype=np.float64) / ROT_DIM)
    ang = pos[:, None] * inv[None, :]
    cos, sin = np.cos(ang), np.sin(ang)
    half = ROT_DIM // 2
    pad = HEAD_DIM - ROT_DIM
    one = np.ones((seq, pad))
    zero = np.zeros((seq, pad))
    zh = np.zeros((seq, half))
    c = np.concatenate([cos, cos, one], axis=1)
    s1 = np.concatenate([-sin, zh, zero], axis=1)
    s2 = np.concatenate([zh, sin, zero], axis=1)
    tile = lambda a: np.tile(a, (1, LANES // HEAD_DIM)).astype(np.float32)
    return tile(c), tile(s1), tile(s2)


def _bias_tables(max_dist, order):
    qi = order[None, :]
    kj = np.concatenate([order, order + BLOCK])[:, None]
    dist = BLOCK + qi - kj
    ok = (dist >= 0) & (dist <= max_dist)
    first = ok & (kj >= BLOCK)
    return np.stack([np.where(first, 0.0, NEG), np.where(ok, 0.0, NEG)]).astype(bf16)


def kernel(x, norm_a, w_in_a, q_gain_a, k_gain_a, sinks_a, w_out_a,
           norm_b, w_in_b, q_gain_b, k_gain_b, w_out_b):
    batch, seq, _ = x.shape
    n = batch * seq
    hid = np.arange(CHUNK) // HEAD_DIM
    bd = np.where(hid[:, None] == hid[None, :], 1.0 / HEAD_DIM, 0.0).astype(bf16)
    lane = np.arange(LANES)[None, :]
    mlo = np.broadcast_to(lane < HEAD_DIM, (2 * BLOCK, LANES)).astype(bf16)
    mhi = np.broadcast_to(lane >= HEAD_DIM, (2 * BLOCK, LANES)).astype(bf16)
    eye = np.eye(BLOCK).astype(bf16)
    natural = np.arange(BLOCK)
    consts_a = (eye, _bias_tables(A_WINDOW - 1, natural), mlo, mhi)

    r = np.arange(TILE)
    dest = RUN * (r % MAX_DIL) + r // MAX_DIL
    perm = (np.arange(TILE)[:, None] == dest[None, :]).astype(bf16)
    unperm = np.ascontiguousarray(perm.T)
    src = MAX_DIL * (r % RUN) + r // RUN
    rope_nat = _rope_tables(seq)
    rope_perm = tuple(t.reshape(seq // TILE, TILE, LANES)[:, src, :].reshape(seq, LANES) for t in rope_nat)
    order = {}
    for _, d in B_PAIRS:
        if d == 1 or d == MAX_DIL:
            order[d] = natural
        else:
            assert BLOCK * d == TILE and MAX_DIL % d == 0
            step = MAX_DIL // d
            order[d] = step * (natural % RUN) + natural // RUN
    consts_b = {d: (eye, _bias_tables(BLOCK, order[d]), mlo, mhi) for _, d in B_PAIRS}

    half = BRANCH // (B_KV // 2)
    col_tile = np.arange(half)[None, :] // LANES
    col_odd = (np.arange(half)[None, :] % LANES) // HEAD_DIM
    row = np.arange(LANES)[:, None]
    ex = ((row % HEAD_DIM == col_tile) & (row // HEAD_DIM == col_odd)).astype(bf16)
    ex = np.concatenate([ex, ex], axis=0)
    tile_gain = lambda g: jnp.tile(g, (1, CHUNK // HEAD_DIM))[:, None, :]
    q_scale = SCALE * LOG2E
    w_in_a, w_out_a, w_in_b, w_out_b = (w.astype(bf16) for w in (w_in_a, w_out_a, w_in_b, w_out_b))

    x2 = x.reshape(n, D_MODEL)
    depth = norm_a.shape[0] + norm_b.shape[0]
    for layer in range(depth):
        idx = layer // 2
        if layer % 2 == 0:
            q, k, ksw, v, vsw, sg = _project(
                x2, norm_a[idx][None], (w_in_a, idx), bd,
                tile_gain(q_gain_a[idx][None] * q_scale), tile_gain(k_gain_a[idx][None]),
                (rope_nat,), None, permuted=(False,), n_kv=A_KV, seq=seq)
            x2 = _attention_dense(
                q, k, ksw, v, vsw, consts_a, batch=batch, seq=seq, n_kv=A_KV,
                gate_args=(sinks_a[idx].astype(f32) * LOG2E, sg, x2, (w_out_a, idx)))
        else:
            permuted = tuple(d > 1 for _, d in B_PAIRS)
            outs = _project(
                x2, norm_b[idx][None], (w_in_b, idx), bd,
                tile_gain(q_gain_b[idx] * q_scale), tile_gain(k_gain_b[idx]),
                (rope_nat, rope_perm), perm, permuted=permuted, n_kv=B_KV, seq=seq)
            sg = outs[-1]
            merged = []
            for g, (window, dilation) in enumerate(B_PAIRS):
                assert window // dilation == BLOCK
                qkv = outs[5 * g:5 * g + 5]
                if dilation == 1:
                    merged += _attention_dense(*qkv, consts_b[1], batch=batch, seq=seq, n_kv=B_KV)
                else:
                    merged += _attention_runs(*qkv, consts_b[dilation], batch=batch, seq=seq,
                                              dilation=dilation, n_kv=B_KV)
            x2 = _merge_project(x2, *merged, sg, ex, unperm, (w_out_b, idx))
    return x2.reshape(batch, seq, D_MODEL)
```

```python
import functools
import math

import jax
import jax.numpy as jnp
import numpy as np
from jax import lax
from jax.experimental import pallas as pl
from jax.experimental.pallas import tpu as pltpu

D_MODEL = 1024
HEAD_DIM = 64
N_HEADS = 16
BRANCH = N_HEADS * HEAD_DIM
A_KV = 2
B_KV = 4
B_PAIRS = ((128, 1), (512, 4), (2048, 16))
A_WINDOW = 128
BLOCK = 128
ROT_DIM = HEAD_DIM // 4
ROPE_THETA = 500000.0
EPS = 1e-6
SCALE = HEAD_DIM ** -0.5
LOG2E = math.log2(math.e)
LANES = 128
Q_TILES = BRANCH // LANES
NEG = -1e30

TILE = 512
ATT_ROWS = 1024
ATT_ROWS_B = 2048
STATIC_BLOCKS = 16
MERGE_TILES = 2
OUT_BLOCKS = 4
CHUNK = 256
WIDE = 512
MAX_DIL = 16
RUN = TILE // MAX_DIL
N_RUNS = BLOCK // RUN
VMEM_LIMIT = 56 * 1024 * 1024

bf16 = jnp.bfloat16
f32 = jnp.float32


def _const_spec(shape):
    nd = len(shape)
    return pl.BlockSpec(shape, lambda *_: (0,) * nd, pipeline_mode=pl.Buffered(1))


def _layer_spec(stacked_and_index):
    stacked, index = stacked_and_index
    return pl.BlockSpec((None,) + stacked.shape[1:], lambda *_: (index, 0, 0), pipeline_mode=pl.Buffered(1))


def _lane_tiles(z):
    return [z[:, i:i + LANES] for i in range(0, z.shape[1], LANES)]


def _proj_kernel(*refs, permuted, n_kv):
    x_ref, ng_ref, w_ref, bd_ref, gq_ref, gk_ref = refs[:6]
    n_tab = 2 if any(permuted) else 1
    tabs = [tuple(r[...] for r in refs[6 + 3 * i:9 + 3 * i]) for i in range(n_tab)]
    pos = 6 + 3 * n_tab
    if any(permuted):
        perm_ref = refs[pos]
        pos += 1
    out_refs = refs[pos:]

    x = x_ref[...]
    ms = jnp.mean(x * x, axis=-1, keepdims=True)
    h = ((x * lax.rsqrt(ms + EPS)) * ng_ref[...]).astype(bf16)
    hs = [h]
    if any(permuted):
        hs.append(jnp.dot(perm_ref[...], h, preferred_element_type=f32).astype(bf16))
    bd = bd_ref[...]

    def roll_tiles(t, shift):
        return jnp.concatenate([pltpu.roll(a, shift, axis=1) for a in _lane_tiles(t)], axis=1)

    def head_norm_rope(z, gain, tab):
        w = z.shape[1]
        cos, sn1, sn2 = (jnp.concatenate([a] * (w // LANES), axis=1) for a in tab)
        msq = jnp.dot((z * z).astype(bf16), bd[:w, :w], preferred_element_type=f32)
        t = (z * lax.rsqrt(msq + EPS)) * gain[:, :w]
        up = roll_tiles(t, LANES - ROT_DIM // 2)
        dn = roll_tiles(t, ROT_DIM // 2)
        return t * cos + up * sn1 + dn * sn2

    kw = n_kv * HEAD_DIM
    col = 0
    oi = 0
    for g, perm in enumerate(permuted):
        hg, tab = hs[int(perm)], tabs[int(perm)]
        q_ref, k_ref, ksw_ref, v_ref, vsw_ref = out_refs[oi:oi + 5]
        oi += 5
        for c in range(0, BRANCH, WIDE):
            zz = jnp.dot(hg, w_ref[:, col + c:col + c + WIDE], preferred_element_type=f32)
            for cc in range(0, WIDE, CHUNK):
                q_ref[:, c + cc:c + cc + CHUNK] = head_norm_rope(
                    zz[:, cc:cc + CHUNK], gq_ref[g], tab).astype(bf16)
        col += BRANCH
        zz = jnp.dot(hg, w_ref[:, col:col + 2 * kw], preferred_element_type=f32)
        k = head_norm_rope(zz[:, :kw], gk_ref[g], tab)
        k_ref[...] = k.astype(bf16)
        ksw_ref[...] = roll_tiles(k, LANES // 2).astype(bf16)
        v = zz[:, kw:]
        v_ref[...] = v.astype(bf16)
        vsw_ref[...] = roll_tiles(v, LANES // 2).astype(bf16)
        col += 2 * kw
    sg_ref = out_refs[oi]
    for c in range(0, BRANCH, WIDE):
        gt = jnp.dot(h, w_ref[:, col + c:col + c + WIDE], preferred_element_type=f32)
        sg_ref[:, c:c + WIDE] = (gt * (1.0 / (1.0 + jnp.exp(-gt)))).astype(bf16)


def _project(x2, norm_g, w, bd, gq, gk, rope_tabs, perm, *, permuted, n_kv, seq):
    n = x2.shape[0]
    kw = n_kv * HEAD_DIM
    rows = TILE if any(permuted) else 2 * TILE
    per_seq = seq // rows
    row_spec = lambda wd: pl.BlockSpec((rows, wd), lambda i: (i, 0))
    rope_spec = pl.BlockSpec((rows, LANES), lambda i: (i % per_seq, 0))
    in_specs = [row_spec(D_MODEL), _const_spec((1, D_MODEL)), _layer_spec(w),
                _const_spec((CHUNK, CHUNK)), _const_spec(gq.shape), _const_spec(gk.shape)]
    args = [x2, norm_g, w[0], bd, gq, gk]
    for tab in rope_tabs[:2 if any(permuted) else 1]:
        in_specs += [rope_spec] * 3
        args += list(tab)
    if any(permuted):
        in_specs.append(_const_spec(perm.shape))
        args.append(perm)
    out_shapes, out_specs = [], []
    for _ in permuted:
        for wd in (BRANCH, kw, kw, kw, kw):
            out_shapes.append(jax.ShapeDtypeStruct((n, wd), bf16))
            out_specs.append(row_spec(wd))
    out_shapes.append(jax.ShapeDtypeStruct((n, BRANCH), bf16))
    out_specs.append(row_spec(BRANCH))
    return pl.pallas_call(
        functools.partial(_proj_kernel, permuted=permuted, n_kv=n_kv),
        grid=(n // rows,),
        in_specs=in_specs,
        out_specs=out_specs,
        out_shape=out_shapes,
        compiler_params=pltpu.CompilerParams(
            dimension_semantics=("parallel",), vmem_limit_bytes=VMEM_LIMIT),
        name="proj",
    )(*args)


def _slab_attention(q_tiles, k2, ks2, v2, vs2, eye, bias_t, mlo, mhi, sinks):
    npair = len(q_tiles) // 2
    lane_lo = lax.broadcasted_iota(jnp.int32, (BLOCK, LANES), 1) < HEAD_DIM
    variants = ((k2 * mlo, ks2 * mhi, v2 * mlo, vs2 * mhi),
                (ks2 * mlo, k2 * mhi, vs2 * mlo, v2 * mhi))
    results = []
    for kv in range(2):
        k_lo, k_hi, v_lo, v_hi = variants[kv]
        kx = (jnp.concatenate([k_lo, bias_t], axis=1), jnp.concatenate([k_hi, bias_t], axis=1))
        vx = (jnp.concatenate([v_lo, mlo], axis=1), jnp.concatenate([v_hi, mhi], axis=1))
        tiles = q_tiles[kv * npair:(kv + 1) * npair]
        qs = jnp.concatenate([jnp.concatenate([t, eye], axis=1) for t in tiles], axis=0)
        acc = None
        m_par, x_par = [], []
        for e in range(2):
            s_all = lax.dot_general(qs, kx[e], (((1,), (1,)), ((), ())), preferred_element_type=f32)
            ps, ms, xs = [], [], []
            for i in range(npair):
                s = s_all[BLOCK * i:BLOCK * (i + 1)].astype(bf16)
                mb = jnp.max(s, axis=-1, keepdims=True)
                if sinks is not None:
                    sk = sinks[kv * npair + i][e]
                    mb = jnp.maximum(mb.astype(f32), sk).astype(bf16)
                m = mb.astype(f32)
                if sinks is not None:
                    xs.append(jnp.exp2(sk - m))
                ps.append(jnp.exp2(s - mb))
                ms.append(m)
            ol = jnp.dot(jnp.concatenate(ps, axis=0), vx[e], preferred_element_type=f32)
            acc = ol if acc is None else acc + ol
            m_par.append(ms)
            x_par.append(xs)
        for i in range(npair):
            a = acc[BLOCK * i:BLOCK * (i + 1)]
            den = a[:, LANES:]
            if sinks is not None:
                den = den + jnp.where(lane_lo, x_par[0][i], x_par[1][i])
            o = a[:, :LANES] / den
            lse2 = jnp.where(lane_lo, m_par[0][i], m_par[1][i]) + jnp.log2(den)
            results.append((o, lse2))
    return results


def _compact_lse(lses):
    lane = lax.broadcasted_iota(jnp.int32, (BLOCK, LANES), 1) % HEAD_DIM
    out = jnp.zeros((BLOCK, LANES), f32)
    for j, l in enumerate(lses):
        out = jnp.where(lane == j, l, out)
    return out


def _attn_dense_kernel(*refs, n_kv, gated):
    (q_ref, kp_ref, kc_ref, ksp_ref, ksc_ref, vp_ref, vc_ref, vsp_ref, vsc_ref,
     eye_ref, bias_ref, mlo_ref, mhi_ref) = refs[:13]
    if gated:
        sink_ref, sg_ref, x_ref, wo_ref, xo_ref, y_buf = refs[13:]
    else:
        o_ref, lse_ref = refs[13:]
    n_slab = n_kv // 2
    tiles_per_slab = Q_TILES // n_slab
    eye, mlo, mhi = eye_ref[...], mlo_ref[...], mhi_ref[...]
    first_step = pl.program_id(1) == 0

    def keys(p_ref, c_ref, i, ls):
        if i == 0:
            return jnp.concatenate([p_ref[:, ls], c_ref[0:BLOCK, ls]], axis=0)
        return c_ref[BLOCK * (i - 1):BLOCK * (i + 1), ls]

    for i in range(q_ref.shape[0] // BLOCK):
        rq = slice(BLOCK * i, BLOCK * (i + 1))
        bias_t = bias_ref[jnp.where(first_step, 0, 1)] if i == 0 else bias_ref[1]
        for sl in range(n_slab):
            ls = slice(LANES * sl, LANES * (sl + 1))
            t0 = sl * tiles_per_slab
            q_tiles = [q_ref[rq, LANES * j:LANES * (j + 1)] for j in range(t0, t0 + tiles_per_slab)]
            sinks = None
            if gated:
                sinks = [(sink_ref[2 * j], sink_ref[2 * j + 1]) for j in range(t0, t0 + tiles_per_slab)]
            res = _slab_attention(q_tiles, keys(kp_ref, kc_ref, i, ls), keys(ksp_ref, ksc_ref, i, ls),
                                  keys(vp_ref, vc_ref, i, ls), keys(vsp_ref, vsc_ref, i, ls),
                                  eye, bias_t, mlo, mhi, sinks)
            for j, (o, lse2) in enumerate(res):
                cs = slice(LANES * (t0 + j), LANES * (t0 + j + 1))
                if gated:
                    y_buf[rq, cs] = (o * sg_ref[rq, cs].astype(f32)).astype(bf16)
                else:
                    o_ref[rq, cs] = o.astype(bf16)
            if not gated:
                lse_ref[sl, rq, :] = _compact_lse([l for _, l in res])
        if gated and (i + 1) % OUT_BLOCKS == 0:
            ro = slice(BLOCK * (i + 1 - OUT_BLOCKS), BLOCK * (i + 1))
            y = y_buf[ro, :]
            for c in range(0, D_MODEL, WIDE):
                xo_ref[ro, c:c + WIDE] = x_ref[ro, c:c + WIDE] + jnp.dot(
                    y, wo_ref[:, c:c + WIDE], preferred_element_type=f32)


def _attention_dense(q, k, ksw, v, vsw, consts, *, batch, seq, n_kv, gate_args=None):
    n = batch * seq
    kw = n_kv * HEAD_DIM
    gated = gate_args is not None
    rows = ATT_ROWS if gated else ATT_ROWS_B
    per_seq = seq // rows
    ratio = rows // BLOCK
    cur = lambda wd: pl.BlockSpec((rows, wd), lambda b, j: (b * per_seq + j, 0))
    prev = lambda wd: pl.BlockSpec(
        (BLOCK, wd), lambda b, j: ((b * per_seq + j) * ratio - jnp.minimum(j, 1), 0))
    eye, bias, mlo, mhi = consts
    in_specs = [cur(BRANCH), prev(kw), cur(kw), prev(kw), cur(kw), prev(kw), cur(kw), prev(kw), cur(kw),
                _const_spec(eye.shape), _const_spec(bias.shape), _const_spec(mlo.shape),
                _const_spec(mhi.shape)]
    args = [q, k, k, ksw, ksw, v, v, vsw, vsw, eye, bias, mlo, mhi]
    scratch = []
    if gated:
        sinks, sg, x2, w_out = gate_args
        in_specs += [pl.BlockSpec(memory_space=pltpu.SMEM), cur(BRANCH), cur(D_MODEL),
                     _layer_spec(w_out)]
        args += [sinks, sg, x2, w_out[0]]
        out_shape = jax.ShapeDtypeStruct((n, D_MODEL), f32)
        out_specs = cur(D_MODEL)
        scratch = scratch + [pltpu.VMEM((rows, BRANCH), bf16)]
    else:
        out_shape = [jax.ShapeDtypeStruct((n, BRANCH), bf16),
                     jax.ShapeDtypeStruct((n_kv // 2, n, LANES), f32)]
        out_specs = [cur(BRANCH),
                     pl.BlockSpec((n_kv // 2, rows, LANES), lambda b, j: (0, b * per_seq + j, 0))]
    return pl.pallas_call(
        functools.partial(_attn_dense_kernel, n_kv=n_kv, gated=gated),
        grid=(batch, per_seq),
        in_specs=in_specs,
        out_specs=out_specs,
        out_shape=out_shape,
        scratch_shapes=scratch,
        compiler_params=pltpu.CompilerParams(
            dimension_semantics=("parallel", "arbitrary"), vmem_limit_bytes=VMEM_LIMIT),
        name="attn_dense_a" if gated else "attn_dense_b",
    )(*args)


def _attn_runs_kernel(q_ref, kp_ref, kc_ref, ksp_ref, ksc_ref, vp_ref, vc_ref, vsp_ref, vsc_ref,
                      eye_ref, bias_ref, mlo_ref, mhi_ref, o_ref, lse_ref, *, dilation, n_kv):
    eye, mlo, mhi = eye_ref[...], mlo_ref[...], mhi_ref[...]
    n_slab = n_kv // 2
    tiles_per_slab = Q_TILES // n_slab
    chunk = BLOCK * dilation
    run_pitch = chunk // N_RUNS

    def residue(c, h):
        def runs_at(base):
            if isinstance(c, int):
                return [slice(base + i * run_pitch + c * RUN, base + i * run_pitch + (c + 1) * RUN)
                        for i in range(N_RUNS)]
            return [pl.ds(pl.multiple_of(base + i * run_pitch + c * RUN, RUN), RUN) for i in range(N_RUNS)]
        cur_runs = runs_at(h * chunk)
        gather = lambda ref, runs, ls: jnp.concatenate([ref[r, ls] for r in runs], axis=0)
        if h == 0:
            both = lambda p_ref, c_ref, ls: jnp.concatenate(
                [gather(p_ref, runs_at(0), ls), gather(c_ref, cur_runs, ls)], axis=0)
            bias_t = bias_ref[jnp.minimum(pl.program_id(1), 1)]
        else:
            both = lambda p_ref, c_ref, ls: jnp.concatenate(
                [gather(c_ref, runs_at((h - 1) * chunk), ls), gather(c_ref, cur_runs, ls)], axis=0)
            bias_t = bias_ref[1]
        for sl in range(n_slab):
            ls = slice(LANES * sl, LANES * (sl + 1))
            t0 = sl * tiles_per_slab
            q_tiles = [gather(q_ref, cur_runs, slice(LANES * j, LANES * (j + 1)))
                       for j in range(t0, t0 + tiles_per_slab)]
            res = _slab_attention(q_tiles, both(kp_ref, kc_ref, ls), both(ksp_ref, ksc_ref, ls),
                                  both(vp_ref, vc_ref, ls), both(vsp_ref, vsc_ref, ls),
                                  eye, bias_t, mlo, mhi, None)
            lse = _compact_lse([l for _, l in res])
            for i, r in enumerate(cur_runs):
                rr = slice(RUN * i, RUN * (i + 1))
                for j, (o, _) in enumerate(res):
                    o_ref[r, LANES * (t0 + j):LANES * (t0 + j + 1)] = o[rr].astype(bf16)
                lse_ref[sl, r, :] = lse[rr]

    n_chunks = q_ref.shape[0] // chunk
    if n_chunks * dilation <= STATIC_BLOCKS:
        for h in range(n_chunks):
            for c in range(dilation):
                residue(c, h)
    else:
        for h in range(n_chunks):
            def body(c, carry, h=h):
                residue(c, h)
                return carry
            lax.fori_loop(0, dilation, body, 0, unroll=STATIC_BLOCKS)


def _attention_runs(q, k, ksw, v, vsw, consts, *, batch, seq, dilation, n_kv):
    n = batch * seq
    kw = n_kv * HEAD_DIM
    chunk = BLOCK * dilation
    rows = max(chunk, ATT_ROWS_B)
    per_seq = seq // rows
    ratio = rows // chunk
    cur = lambda wd: pl.BlockSpec((rows, wd), lambda b, j: (b * per_seq + j, 0))
    prev = lambda wd: pl.BlockSpec(
        (chunk, wd), lambda b, j: ((b * per_seq + j) * ratio - jnp.minimum(j, 1), 0))
    eye, bias, mlo, mhi = consts
    return pl.pallas_call(
        functools.partial(_attn_runs_kernel, dilation=dilation, n_kv=n_kv),
        grid=(batch, per_seq),
        in_specs=[cur(BRANCH), prev(kw), cur(kw), prev(kw), cur(kw), prev(kw), cur(kw), prev(kw), cur(kw),
                  _const_spec(eye.shape), _const_spec(bias.shape), _const_spec(mlo.shape),
                  _const_spec(mhi.shape)],
        out_specs=[cur(BRANCH),
                   pl.BlockSpec((n_kv // 2, rows, LANES), lambda b, j: (0, b * per_seq + j, 0))],
        out_shape=[jax.ShapeDtypeStruct((n, BRANCH), bf16),
                   jax.ShapeDtypeStruct((n_kv // 2, n, LANES), f32)],
        compiler_params=pltpu.CompilerParams(
            dimension_semantics=("parallel", "arbitrary"), vmem_limit_bytes=VMEM_LIMIT),
        name=f"attn_runs_d{dilation}",
    )(q, k, k, ksw, ksw, v, v, vsw, vsw, eye, bias, mlo, mhi)


def _merge_kernel(x_ref, o0_ref, l0_ref, o1_ref, l1_ref, o2_ref, l2_ref, sg_ref, ex_ref, unperm_ref,
                  w_ref, o_ref, lnat):
    n_slab = l0_ref.shape[0]
    o12 = []
    for t0 in range(0, x_ref.shape[0], TILE):
        rt = slice(t0, t0 + TILE)
        o12.append(jnp.dot(unperm_ref[...], jnp.concatenate([o1_ref[rt, :], o2_ref[rt, :]], axis=1),
                           preferred_element_type=f32))
        for gi, l_ref in enumerate((l1_ref, l2_ref)):
            for s in range(n_slab):
                for c in range(MAX_DIL):
                    lnat[gi * n_slab + s, pl.ds(t0 + c, RUN, stride=MAX_DIL), :] = (
                        l_ref[s, t0 + RUN * c:t0 + RUN * (c + 1), :])
    o12 = jnp.concatenate(o12, axis=0)
    ex = ex_ref[...]
    wts = [[], []]
    for s in range(n_slab):
        l0, l1, l2 = l0_ref[s], lnat[s], lnat[n_slab + s]
        mx = jnp.maximum(jnp.maximum(l0, l1), l2)
        e = [jnp.exp2(l0 - mx), jnp.exp2(l1 - mx), jnp.exp2(l2 - mx)]
        inv = 1.0 / (e[0] + e[1] + e[2])
        for g in range(2):
            wg = e[g] * inv
            hi = wg.astype(bf16)
            lo = (wg - hi.astype(f32)).astype(bf16)
            wts[g].append(jnp.dot(jnp.concatenate([hi, lo], axis=1), ex, preferred_element_type=f32))
    w0, w1 = (jnp.concatenate(w, axis=1) for w in wts)
    w2 = 1.0 - w0 - w1
    y = ((w0 * o0_ref[...].astype(f32) + w1 * o12[:, :BRANCH] + w2 * o12[:, BRANCH:])
         * sg_ref[...].astype(f32)).astype(bf16)
    for c in range(0, D_MODEL, WIDE):
        o_ref[:, c:c + WIDE] = x_ref[:, c:c + WIDE] + jnp.dot(
            y, w_ref[:, c:c + WIDE], preferred_element_type=f32)


def _merge_project(x2, o0, l0, o1, l1, o2, l2, sg, ex, unperm, w):
    n = x2.shape[0]
    rows = MERGE_TILES * TILE
    row = pl.BlockSpec((rows, D_MODEL), lambda i: (i, 0))
    slab = pl.BlockSpec((l0.shape[0], rows, LANES), lambda i: (0, i, 0))
    return pl.pallas_call(
        _merge_kernel,
        grid=(n // rows,),
        in_specs=[row, row, slab, row, slab, row, slab, row, _const_spec(ex.shape),
                  _const_spec(unperm.shape), _layer_spec(w)],
        out_specs=row,
        out_shape=jax.ShapeDtypeStruct((n, D_MODEL), f32),
        scratch_shapes=[pltpu.VMEM((2 * l0.shape[0], rows, LANES), f32)],
        input_output_aliases={0: 0},
        compiler_params=pltpu.CompilerParams(
            dimension_semantics=("parallel",), vmem_limit_bytes=VMEM_LIMIT),
        name="merge_outproj",
    )(x2, o0, l0, o1, l1, o2, l2, sg, ex, unperm, w[0])


def _rope_tables(seq):
    pos = np.arange(seq, dtype=np.float64)
    inv = ROPE_THETA ** (-np.arange(0, ROT_DIM, 2, dtype=np.float64) / ROT_DIM)
    ang = pos[:, None] * inv[None, :]
    cos, sin = np.cos(ang), np.sin(ang)
    half = ROT_DIM // 2
    pad = HEAD_DIM - ROT_DIM
    one = np.ones((seq, pad))
    zero = np.zeros((seq, pad))
    zh = np.zeros((seq, half))
    c = np.concatenate([cos, cos, one], axis=1)
    s1 = np.concatenate([-sin, zh, zero], axis=1)
    s2 = np.concatenate([zh, sin, zero], axis=1)
    tile = lambda a: np.tile(a, (1, LANES // HEAD_DIM)).astype(np.float32)
    return tile(c), tile(s1), tile(s2)


def _bias_tables(max_dist, order):
    qi = order[None, :]
    kj = np.concatenate([order, order + BLOCK])[:, None]
    dist = BLOCK + qi - kj
    ok = (dist >= 0) & (dist <= max_dist)
    first = ok & (kj >= BLOCK)
    return np.stack([np.where(first, 0.0, NEG), np.where(ok, 0.0, NEG)]).astype(bf16)


def kernel(x, norm_a, w_in_a, q_gain_a, k_gain_a, sinks_a, w_out_a,
           norm_b, w_in_b, q_gain_b, k_gain_b, w_out_b):
    batch, seq, _ = x.shape
    n = batch * seq
    hid = np.arange(CHUNK) // HEAD_DIM
    bd = np.where(hid[:, None] == hid[None, :], 1.0 / HEAD_DIM, 0.0).astype(bf16)
    lane = np.arange(LANES)[None, :]
    mlo = np.broadcast_to(lane < HEAD_DIM, (2 * BLOCK, LANES)).astype(bf16)
    mhi = np.broadcast_to(lane >= HEAD_DIM, (2 * BLOCK, LANES)).astype(bf16)
    eye = np.eye(BLOCK).astype(bf16)
    natural = np.arange(BLOCK)
    consts_a = (eye, _bias_tables(A_WINDOW - 1, natural), mlo, mhi)

    r = np.arange(TILE)
    dest = RUN * (r % MAX_DIL) + r // MAX_DIL
    perm = (np.arange(TILE)[:, None] == dest[None, :]).astype(bf16)
    unperm = np.ascontiguousarray(perm.T)
    src = MAX_DIL * (r % RUN) + r // RUN
    rope_nat = _rope_tables(seq)
    rope_perm = tuple(t.reshape(seq // TILE, TILE, LANES)[:, src, :].reshape(seq, LANES) for t in rope_nat)
    order = {}
    for _, d in B_PAIRS:
        if d == 1 or d == MAX_DIL:
            order[d] = natural
        else:
            assert BLOCK * d == TILE and MAX_DIL % d == 0
            step = MAX_DIL // d
            order[d] = step * (natural % RUN) + natural // RUN
    consts_b = {d: (eye, _bias_tables(BLOCK, order[d]), mlo, mhi) for _, d in B_PAIRS}

    half = BRANCH // (B_KV // 2)
    col_tile = np.arange(half)[None, :] // LANES
    col_odd = (np.arange(half)[None, :] % LANES) // HEAD_DIM
    row = np.arange(LANES)[:, None]
    ex = ((row % HEAD_DIM == col_tile) & (row // HEAD_DIM == col_odd)).astype(bf16)
    ex = np.concatenate([ex, ex], axis=0)
    tile_gain = lambda g: jnp.tile(g, (1, CHUNK // HEAD_DIM))[:, None, :]
    q_scale = SCALE * LOG2E
    w_in_a, w_out_a, w_in_b, w_out_b = (w.astype(bf16) for w in (w_in_a, w_out_a, w_in_b, w_out_b))

    x2 = x.reshape(n, D_MODEL)
    depth = norm_a.shape[0] + norm_b.shape[0]
    for layer in range(depth):
        idx = layer // 2
        if layer % 2 == 0:
            q, k, ksw, v, vsw, sg = _project(
                x2, norm_a[idx][None], (w_in_a, idx), bd,
                tile_gain(q_gain_a[idx][None] * q_scale), tile_gain(k_gain_a[idx][None]),
                (rope_nat,), None, permuted=(False,), n_kv=A_KV, seq=seq)
            x2 = _attention_dense(
                q, k, ksw, v, vsw, consts_a, batch=batch, seq=seq, n_kv=A_KV,
                gate_args=(sinks_a[idx].astype(f32) * LOG2E, sg, x2, (w_out_a, idx)))
        else:
            permuted = tuple(d > 1 for _, d in B_PAIRS)
            outs = _project(
                x2, norm_b[idx][None], (w_in_b, idx), bd,
                tile_gain(q_gain_b[idx] * q_scale), tile_gain(k_gain_b[idx]),
                (rope_nat, rope_perm), perm, permuted=permuted, n_kv=B_KV, seq=seq)
            sg = outs[-1]
            merged = []
            for g, (window, dilation) in enumerate(B_PAIRS):
                assert window // dilation == BLOCK
                qkv = outs[5 * g:5 * g + 5]
                if dilation == 1:
                    merged += _attention_dense(*qkv, consts_b[1], batch=batch, seq=seq, n_kv=B_KV)
                else:
                    merged += _attention_runs(*qkv, consts_b[dilation], batch=batch, seq=seq,
                                              dilation=dilation, n_kv=B_KV)
            x2 = _merge_project(x2, *merged, sg, ex, unperm, (w_out_b, idx))
    return x2.reshape(batch, seq, D_MODEL)
```
